```python
import math
import jax, jax.numpy as jnp
from jax import lax
import numpy as np

D_MODEL = 1024
BATCH = 32
SEQ = 2048
DEPTH = 1
DEC_BATCH = 128
DEC_SEQ = 8
PAST_LEN = 8192
PAGE_SIZE = 128

N_HEADS = 4
D_QK = 64
D_V = 2 * D_QK
ATT_W = N_HEADS * D_V
POOL_WINDOWS = (2, 4, 8, 16)
N_POOL_GROUPS = len(POOL_WINDOWS)
POOL_W = D_MODEL - ATT_W
POOL_G = POOL_W // N_POOL_GROUPS
POOL_STATE = max(POOL_WINDOWS) - 1
Q_W = N_HEADS * 2 * D_QK
IN_W = 2 * Q_W + ATT_W + POOL_W
ROT_DIM = D_QK // 4
ROPE_THETA = 500000.0
D_FF = 2816
CONV_W = 3
EPS = 1e-6
Q_BLOCK = 128
SCALE = D_QK ** -0.5

kernel_name = "hymba_diffattn_pool_convffn_adaln_step"


def rmsnorm(x, g):
    xf = x.astype(jnp.float32)
    y = xf * lax.rsqrt(jnp.mean(xf * xf, axis=-1, keepdims=True) + EPS)
    return (y * g.astype(jnp.float32)).astype(x.dtype)


def rope(x, pos):
    half = ROT_DIM // 2
    inv = ROPE_THETA ** (-jnp.arange(0, ROT_DIM, 2, dtype=jnp.float32) / ROT_DIM)
    ang = pos.astype(jnp.float32)[:, None] * inv[None, :]
    cos = jnp.cos(ang)[None, :, None, None, :]
    sin = jnp.sin(ang)[None, :, None, None, :]
    xr = x[..., :ROT_DIM].astype(jnp.float32)
    x1, x2 = xr[..., :half], xr[..., half:]
    rot = jnp.concatenate([x1 * cos - x2 * sin, x2 * cos + x1 * sin], axis=-1).astype(x.dtype)
    return jnp.concatenate([rot, x[..., ROT_DIM:]], axis=-1)


def adaln(c, w_ada, b_ada):
    m = jax.nn.silu(c) @ w_ada + b_ada
    return [t[:, None, :] for t in jnp.split(m, 6, axis=-1)]


def project(h, pos, w_in, g_q, g_k):
    B, T, _ = h.shape
    p = h @ w_in
    q, k, v, u = jnp.split(p, [Q_W, 2 * Q_W, 2 * Q_W + ATT_W], axis=-1)
    q = rope(rmsnorm(q.reshape(B, T, N_HEADS, 2, D_QK), g_q), pos)
    k = rope(rmsnorm(k.reshape(B, T, N_HEADS, 2, D_QK), g_k), pos)
    return q, k.reshape(B, T, N_HEADS, 2 * D_QK), v.reshape(B, T, N_HEADS, D_V), u


def prompt_attention(q, k, v, lam):
    B, S = q.shape[:2]
    nb = S // Q_BLOCK
    k5 = k.reshape(B, S, N_HEADS, 2, D_QK)
    qb = jnp.moveaxis(q.reshape(B, nb, Q_BLOCK, N_HEADS, 2, D_QK), 1, 0)
    kpos = jnp.arange(S)

    def block(args):
        qblk, i = args
        qpos = i * Q_BLOCK + jnp.arange(Q_BLOCK)
        mask = kpos[None, :] <= qpos[:, None]
        s = jnp.einsum('bqhmd,bkhmd->mbhqk', qblk, k5).astype(jnp.float32) * SCALE
        p = jax.nn.softmax(jnp.where(mask, s, -jnp.inf), axis=-1)
        w = (p[0] - lam * p[1]).astype(v.dtype)
        return jnp.einsum('bhqk,bkhd->bqhd', w, v)

    o = lax.map(block, (qb, jnp.arange(nb)))
    return jnp.moveaxis(o, 0, 1).reshape(B, S, N_HEADS, D_V)


def sample_attention(q, k, v, k_past, v_past, lam):
    B, T = q.shape[:2]
    L = k_past.shape[1]
    kp = k_past.reshape(B, L, N_HEADS, 2, D_QK)
    kn = k.reshape(B, T, N_HEADS, 2, D_QK)
    s_past = jnp.einsum('bqhmd,bkhmd->mbhqk', q, kp).astype(jnp.float32) * SCALE
    s_new = jnp.einsum('bqhmd,bkhmd->mbhqk', q, kn).astype(jnp.float32) * SCALE
    causal = jnp.arange(T)[None, :] <= jnp.arange(T)[:, None]
    s_new = jnp.where(causal, s_new, -jnp.inf)
    p = jax.nn.softmax(jnp.concatenate([s_past, s_new], axis=-1), axis=-1)
    w = (p[0] - lam * p[1]).astype(v.dtype)
    return (jnp.einsum('bhqk,bkhd->bqhd', w[..., :L], v_past)
            + jnp.einsum('bhqk,bkhd->bqhd', w[..., L:], v))


def pool_mix(u, prefix, pos0, w_pool, pool_scale):
    B, T, _ = u.shape
    ext_raw = jnp.concatenate([prefix, u], axis=1)
    ext = ext_raw.astype(jnp.float32)
    cs = jnp.concatenate([jnp.zeros((B, 1, POOL_W), jnp.float32), jnp.cumsum(ext, axis=1)], axis=1)
    pos = pos0 + jnp.arange(T)
    end = cs[:, POOL_STATE + 1:POOL_STATE + 1 + T]
    means = []
    for g, w in enumerate(POOL_WINDOWS):
        sl = slice(g * POOL_G, (g + 1) * POOL_G)
        start = cs[:, POOL_STATE + 1 - w:POOL_STATE + 1 - w + T, sl]
        cnt = jnp.minimum(w, pos + 1).astype(jnp.float32)[None, :, None]
        means.append((end[..., sl] - start) / cnt)
    pooled = (jnp.concatenate(means, axis=-1) - ext[:, POOL_STATE:]).astype(u.dtype)
    y = jnp.einsum('btgc,gcd->btgd', pooled.reshape(B, T, N_POOL_GROUPS, POOL_G), w_pool)
    return y.reshape(B, T, POOL_W) * pool_scale, ext_raw[:, -POOL_STATE:]


def conv_ffn(h, prefix, w_up, conv_w, conv_b, w_down):
    a = h @ w_up
    T = a.shape[1]
    ext = jnp.concatenate([prefix, a], axis=1)
    y = conv_b
    for j in range(CONV_W):
        y = y + ext[:, j:j + T] * conv_w[j]
    g, up = jnp.split(y, 2, axis=-1)
    return (jax.nn.silu(g) * up) @ w_down, ext[:, -(CONV_W - 1):]


def layer(x, c, pos, pool_prefix, conv_prefix, attend, lam_init,
          w_ada, b_ada, g_attn, w_in, g_q, g_k, g_sub, w_pool, pool_scale, w_out,
          g_ffn, w_up, conv_w, conv_b, w_down):
    B, T, _ = x.shape
    sh_a, sc_a, ga_a, sh_m, sc_m, ga_m = adaln(c, w_ada, b_ada)
    h = rmsnorm(x, g_attn) * (1 + sc_a) + sh_a
    q, k, v, u = project(h, pos, w_in, g_q, g_k)
    o_att = rmsnorm(attend(q, k, v), g_sub) * (1.0 - lam_init)
    o_pool, pool_state = pool_mix(u, pool_prefix, pos[0], w_pool, pool_scale)
    mix = jnp.concatenate([o_att.reshape(B, T, ATT_W), o_pool], axis=-1) @ w_out
    x = x + ga_a * mix
    h2 = rmsnorm(x, g_ffn) * (1 + sc_m) + sh_m
    f, conv_state = conv_ffn(h2, conv_prefix, w_up, conv_w, conv_b, w_down)
    x = x + ga_m * f
    return x, k, v, pool_state, conv_state


def setup_inputs(seed: int = 0) -> dict:
    key = jax.random.key(seed)
    ks = jax.random.split(key, 32)
    f32 = jnp.float32
    n_pages = PAST_LEN // PAGE_SIZE
    n_pool = (DEC_BATCH * n_pages * 5) // 4
    nrm = lambda k, s, sc: jax.random.normal(k, s, f32) * sc
    page_table = jax.random.permutation(ks[0], n_pool)[:DEC_BATCH * n_pages]
    page_table = page_table.reshape(DEC_BATCH, n_pages).astype(jnp.int32)
    return {
        "x_prompt": nrm(ks[1], (BATCH, SEQ, D_MODEL), 1.0),
        "x_sample": nrm(ks[2], (DEC_BATCH, DEC_SEQ, D_MODEL), 1.0),
        "cache_k": nrm(ks[3], (DEPTH, n_pool, PAGE_SIZE, N_HEADS, 2 * D_QK), 1.0),
        "cache_v": nrm(ks[4], (DEPTH, n_pool, PAGE_SIZE, N_HEADS, D_V), 1.0),
        "state_pool": nrm(ks[5], (DEPTH, DEC_BATCH, POOL_STATE, POOL_W), 1.0),
        "state_conv": nrm(ks[6], (DEPTH, DEC_BATCH, CONV_W - 1, 2 * D_FF), 1.0),
        "page_table": page_table,
        "c_prompt": nrm(ks[7], (BATCH, D_MODEL), 1.0),
        "c_sample": nrm(ks[8], (DEC_BATCH, D_MODEL), 1.0),
        "w_ada": nrm(ks[9], (DEPTH, D_MODEL, 6 * D_MODEL), 0.5 * D_MODEL ** -0.5),
        "b_ada": nrm(ks[10], (DEPTH, 6 * D_MODEL), 0.01),
        "g_attn": 1.0 + nrm(ks[11], (DEPTH, D_MODEL), 0.05),
        "w_in": nrm(ks[12], (DEPTH, D_MODEL, IN_W), D_MODEL ** -0.5),
        "g_q": 1.0 + nrm(ks[13], (DEPTH, D_QK), 0.05),
        "g_k": 1.0 + nrm(ks[14], (DEPTH, D_QK), 0.05),
        "lam_q1": nrm(ks[15], (DEPTH, D_QK), 0.1),
        "lam_k1": nrm(ks[16], (DEPTH, D_QK), 0.1),
        "lam_q2": nrm(ks[17], (DEPTH, D_QK), 0.1),
        "lam_k2": nrm(ks[18], (DEPTH, D_QK), 0.1),
        "g_sub": 1.0 + nrm(ks[19], (DEPTH, D_V), 0.05),
        "w_pool": nrm(ks[20], (DEPTH, N_POOL_GROUPS, POOL_G, POOL_G), POOL_G ** -0.5),
        "pool_scale": 1.0 + nrm(ks[21], (DEPTH, POOL_W), 0.1),
        "w_out": nrm(ks[22], (DEPTH, D_MODEL, D_MODEL), D_MODEL ** -0.5),
        "g_ffn": 1.0 + nrm(ks[23], (DEPTH, D_MODEL), 0.05),
        "w_up": nrm(ks[24], (DEPTH, D_MODEL, 2 * D_FF), D_MODEL ** -0.5),
        "conv_w": nrm(ks[25], (DEPTH, CONV_W, 2 * D_FF), CONV_W ** -0.5),
        "conv_b": nrm(ks[26], (DEPTH, 2 * D_FF), 0.01),
        "w_down": nrm(ks[27], (DEPTH, D_FF, D_MODEL), D_FF ** -0.5),
    }


def reference(x_prompt, x_sample, cache_k, cache_v, state_pool, state_conv, page_table,
              c_prompt, c_sample, w_ada, b_ada, g_attn, w_in, g_q, g_k,
              lam_q1, lam_k1, lam_q2, lam_k2, g_sub, w_pool, pool_scale, w_out,
              g_ffn, w_up, conv_w, conv_b, w_down):
    B, S, _ = x_prompt.shape
    Bd, T, _ = x_sample.shape
    past_len = page_table.shape[1] * cache_k.shape[2]
    pos_p = jnp.arange(S)
    pos_s = past_len + jnp.arange(T)
    yp, ys = x_prompt, x_sample
    nkp, nvp, npp, ncp, nks, nvs, nps, ncs = [], [], [], [], [], [], [], []
    for l in range(DEPTH):
        lam_init = 0.8 - 0.6 * math.exp(-0.3 * l)
        lam = (jnp.exp(jnp.sum(lam_q1[l].astype(jnp.float32) * lam_k1[l].astype(jnp.float32)))
               - jnp.exp(jnp.sum(lam_q2[l].astype(jnp.float32) * lam_k2[l].astype(jnp.float32)))
               + lam_init)
        lp = (w_ada[l], b_ada[l], g_attn[l], w_in[l], g_q[l], g_k[l], g_sub[l], w_pool[l],
              pool_scale[l], w_out[l], g_ffn[l], w_up[l], conv_w[l], conv_b[l], w_down[l])
        yp, kp, vp, pp, cp = layer(
            yp, c_prompt, pos_p,
            jnp.zeros((B, POOL_STATE, POOL_W), yp.dtype),
            jnp.zeros((B, CONV_W - 1, 2 * D_FF), yp.dtype),
            lambda q, k, v: prompt_attention(q, k, v, lam), lam_init, *lp)
        k_past = cache_k[l, page_table].reshape(Bd, past_len, N_HEADS, 2 * D_QK)
        v_past = cache_v[l, page_table].reshape(Bd, past_len, N_HEADS, D_V)
        ys, ks_, vs_, ps_, cs_ = layer(
            ys, c_sample, pos_s, state_pool[l], state_conv[l],
            lambda q, k, v: sample_attention(q, k, v, k_past, v_past, lam), lam_init, *lp)
        nkp.append(kp); nvp.append(vp); npp.append(pp); ncp.append(cp)
        nks.append(ks_); nvs.append(vs_); nps.append(ps_); ncs.append(cs_)
    return (yp, ys,
            jnp.stack(nkp), jnp.stack(nvp), jnp.stack(npp), jnp.stack(ncp),
            jnp.stack(nks), jnp.stack(nvs), jnp.stack(nps), jnp.stack(ncs))
```

```python
import functools
import math

import jax
import jax.numpy as jnp
from jax import lax
from jax.experimental import pallas as pl
from jax.experimental.pallas import tpu as pltpu

F32 = jnp.float32
BF16 = jnp.bfloat16

POOL_WINDOWS = (2, 4, 8, 16)
POOL_PREFIX_ROWS = 16
CONV_W = 3
CONV_PREFIX_ROWS = 8
ROPE_THETA = 500000.0
EPS = 1e-6
LANES = 128
NEG_BIG = -1e30
VMEM_LIMIT = 56 * 1024 * 1024


def _silu(x):
    return x / (1.0 + jnp.exp(-x))


def _rms(x, axis=-1):
    return x * lax.rsqrt(jnp.mean(x * x, axis=axis, keepdims=True) + EPS)


def _dot(a, b):
    return jnp.dot(a, b, preferred_element_type=F32)


def _dot_nt(a, b):
    return lax.dot_general(a, b, (((1,), (1,)), ((), ())), preferred_element_type=F32)


def _split_dot(x, w_bf16):
    hi = x.astype(BF16)
    lo = (x - hi.astype(F32)).astype(BF16)
    return _dot(hi, w_bf16) + _dot(lo, w_bf16)


def _ada_kernel(c_ref, w_ref, b_ref, o_ref):
    s = _silu(c_ref[...]).astype(BF16)
    o_ref[...] = _dot(s, w_ref[...].astype(BF16)) + b_ref[...]


def _ada(c, w_ada, b_ada):
    nb, d = c.shape
    n = w_ada.shape[1]
    bn = 1024 if n % 1024 == 0 else n
    return pl.pallas_call(
        _ada_kernel,
        grid=(n // bn,),
        in_specs=[pl.BlockSpec((nb, d), lambda j: (0, 0)),
                  pl.BlockSpec((d, bn), lambda j: (0, j)),
                  pl.BlockSpec((1, bn), lambda j: (0, j))],
        out_specs=pl.BlockSpec((nb, bn), lambda j: (0, j)),
        out_shape=jax.ShapeDtypeStruct((nb, n), F32),
        compiler_params=pltpu.CompilerParams(dimension_semantics=("arbitrary",),
                                             vmem_limit_bytes=VMEM_LIMIT),
        name="ada",
    )(c, w_ada, b_ada.reshape(1, n))


def _pre_kernel(*refs, nseq, T, n_heads, att_w, pool_w, pos_base, has_prefix):
    if has_prefix:
        (x_ref, mod_ref, gattn_ref, win_ref, gq_ref, gk_ref, gmat_ref, cos_ref, sa_ref, sb_ref,
         wpool_ref, pscale_ref, prefix_ref,
         q_ref, k_ref, kb_ref, v_ref, vb_ref, op_ref, pst_ref) = refs
    else:
        (x_ref, mod_ref, gattn_ref, win_ref, gq_ref, gk_ref, gmat_ref, cos_ref, sa_ref, sb_ref,
         wpool_ref, pscale_ref,
         q_ref, k_ref, kb_ref, v_ref, vb_ref, op_ref, pst_ref) = refs
    j = pl.program_id(1)
    M = nseq * T
    q_w = n_heads * LANES

    x = x_ref[...]
    d = x.shape[-1]
    h = _rms(x) * gattn_ref[...] * (1.0 + mod_ref[:, 1:2, :]) + mod_ref[:, 0:1, :]
    p = _dot(h.reshape(M, d).astype(BF16), win_ref[...])

    cos = jnp.broadcast_to(cos_ref[...][None], (nseq, T, LANES)).reshape(M, LANES)
    sa = jnp.broadcast_to(sa_ref[...][None], (nseq, T, LANES)).reshape(M, LANES)
    sb = jnp.broadcast_to(sb_ref[...][None], (nseq, T, LANES)).reshape(M, LANES)

    def qk_norm_rope(t, g):
        ms = _split_dot(t * t, gmat_ref[...])
        t = t * lax.rsqrt(ms + EPS) * g
        outs = []
        for hh in range(n_heads):
            th = t[:, hh * LANES:(hh + 1) * LANES]
            half = 8
            outs.append(th * cos + pltpu.roll(th, LANES - half, axis=1) * sa
                        + pltpu.roll(th, half, axis=1) * sb)
        return jnp.concatenate(outs, axis=1)

    q = qk_norm_rope(p[:, 0:q_w], gq_ref[...])
    k = qk_norm_rope(p[:, q_w:2 * q_w], gk_ref[...])
    v = p[:, 2 * q_w:2 * q_w + att_w]
    u = p[:, 2 * q_w + att_w:2 * q_w + att_w + pool_w]

    scale = (LANES // 2) ** -0.5
    q_ref[...] = (q * scale).astype(BF16)
    k_ref[...] = k
    kb_ref[...] = k.astype(BF16)
    v_ref[...] = v
    vb_ref[...] = v.astype(BF16)

    P = POOL_PREFIX_ROWS
    if has_prefix:
        pre = prefix_ref[...]
    else:
        @pl.when(j == 0)
        def _():
            pst_ref[...] = jnp.zeros_like(pst_ref)
        pre = pst_ref[...]
    ext3 = jnp.concatenate([pre, u.reshape(nseq, T, pool_w)], axis=1)
    pst_ref[...] = ext3[:, T:T + P, :]
    ext = ext3.reshape(nseq * (P + T), pool_w)

    pos = pos_base + j * T + lax.broadcasted_iota(jnp.int32, (nseq, T, LANES), 1)
    g_w = pool_w // len(POOL_WINDOWS)
    outs = []
    for g, w in enumerate(POOL_WINDOWS):
        e = ext[:, g * g_w:(g + 1) * g_w]
        s = e
        sh = 1
        while sh < w:
            s = s + pltpu.roll(s, sh, axis=0)
            sh *= 2
        s3 = s.reshape(nseq, P + T, g_w)[:, P:, :]
        e3 = e.reshape(nseq, P + T, g_w)[:, P:, :]
        cnt = jnp.minimum(w, pos + 1).astype(F32)
        pooled = (s3 / cnt - e3).reshape(M, g_w).astype(BF16)
        outs.append(_dot(pooled, wpool_ref[g]))
    op_ref[...] = (jnp.concatenate(outs, axis=1) * pscale_ref[...]).astype(BF16)


def _pre(x, mod, g_attn, w_in, gq_t, gk_t, gmat, cos, sa, sb, w_pool, pool_scale, prefix,
         *, nseq, T, n_heads, pos_base):
    ns_total, t_total, d = x.shape
    in_w = w_in.shape[1]
    pool_w = pool_scale.shape[-1]
    q_w = n_heads * LANES
    att_w = in_w - 2 * q_w - pool_w
    assert pool_w // len(POOL_WINDOWS) == LANES and att_w == q_w
    ni, nj = ns_total // nseq, t_total // T
    M = nseq * T
    n_tok = ns_total * t_total
    has_prefix = prefix is not None
    assert has_prefix or nseq == 1

    const = lambda shape: pl.BlockSpec(shape, lambda i, j: (0,) * len(shape),
                                       pipeline_mode=pl.Buffered(1))
    tok = lambda w: pl.BlockSpec((M, w), lambda i, j: (i * nj + j, 0))
    in_specs = [
        pl.BlockSpec((nseq, T, d), lambda i, j: (i, j, 0)),
        pl.BlockSpec((nseq, 6, d), lambda i, j: (i, 0, 0)),
        const((1, d)), const((d, in_w)), const((1, q_w)), const((1, q_w)), const((q_w, q_w)),
        pl.BlockSpec((T, LANES), lambda i, j: (j, 0)),
        pl.BlockSpec((T, LANES), lambda i, j: (j, 0)),
        pl.BlockSpec((T, LANES), lambda i, j: (j, 0)),
        const(w_pool.shape), const((1, pool_w)),
    ]
    args = [x, mod, g_attn.reshape(1, d), w_in, gq_t, gk_t, gmat, cos, sa, sb,
            w_pool, pool_scale.reshape(1, pool_w)]
    if has_prefix:
        in_specs.append(pl.BlockSpec((nseq, POOL_PREFIX_ROWS, pool_w), lambda i, j: (i, 0, 0)))
        args.append(prefix)
    out_shape = (
        jax.ShapeDtypeStruct((n_tok, q_w), BF16),
        jax.ShapeDtypeStruct((n_tok, q_w), F32),
        jax.ShapeDtypeStruct((n_tok, q_w), BF16),
        jax.ShapeDtypeStruct((n_tok, att_w), F32),
        jax.ShapeDtypeStruct((n_tok, att_w), BF16),
        jax.ShapeDtypeStruct((n_tok, pool_w), BF16),
        jax.ShapeDtypeStruct((ns_total, POOL_PREFIX_ROWS, pool_w), F32),
    )
    out_specs = (tok(q_w), tok(q_w), tok(q_w), tok(att_w), tok(att_w), tok(pool_w),
                 pl.BlockSpec((nseq, POOL_PREFIX_ROWS, pool_w), lambda i, j: (i, 0, 0)))
    kern = functools.partial(_pre_kernel, nseq=nseq, T=T, n_heads=n_heads, att_w=att_w,
                             pool_w=pool_w, pos_base=pos_base, has_prefix=has_prefix)
    return pl.pallas_call(
        kern, grid=(ni, nj), in_specs=in_specs, out_specs=out_specs, out_shape=out_shape,
        compiler_params=pltpu.CompilerParams(dimension_semantics=("arbitrary", "arbitrary"),
                                             vmem_limit_bytes=VMEM_LIMIT),
        name="pre",
    )(*args)


def _lam_from(lamvec_ref, lam_init):
    a = jnp.sum(lamvec_ref[0:1, :] * lamvec_ref[1:2, :], axis=-1, keepdims=True)
    b = jnp.sum(lamvec_ref[2:3, :] * lamvec_ref[3:4, :], axis=-1, keepdims=True)
    return jnp.exp(a) - jnp.exp(b) + lam_init


def _attp_kernel(qi_tab, ki_tab, lamvec_ref, q_ref, k_ref, v_ref, gsub_ref, o_ref,
                 qs_ref, m_ref, l_ref, acc_ref, *, tq, lam_init):
    pidx = pl.program_id(2)
    qi = qi_tab[pidx]
    ki = ki_tab[pidx]
    half = LANES // 2

    @pl.when(ki == 0)
    def _():
        q = q_ref[...]
        lane = lax.broadcasted_iota(jnp.int32, q.shape, 1)
        zero = jnp.zeros_like(q)
        qs_ref[0:tq, :] = jnp.where(lane < half, q, zero)
        qs_ref[tq:2 * tq, :] = jnp.where(lane >= half, q, zero)
        m_ref[...] = jnp.full_like(m_ref, NEG_BIG)
        l_ref[...] = jnp.zeros_like(l_ref)
        acc_ref[...] = jnp.zeros_like(acc_ref)

    def update(s):
        tk = s.shape[1]
        m_prev = m_ref[...]
        m_cur = jnp.max(s, axis=1, keepdims=True)
        m_new = jnp.maximum(m_prev, m_cur)
        alpha = jnp.exp(m_prev - m_new)
        p = jnp.exp(s - pltpu.repeat(m_new, tk // LANES, axis=1))
        l_ref[...] = alpha * l_ref[...] + jnp.sum(p, axis=1, keepdims=True)
        acc_ref[...] = alpha * acc_ref[...] + _dot(p.astype(BF16), v_ref[...])
        m_ref[...] = m_new

    @pl.when(ki < qi)
    def _():
        update(_dot_nt(qs_ref[...], k_ref[...]))

    @pl.when(ki == qi)
    def _():
        s = _dot_nt(qs_ref[...], k_ref[...])
        row = lax.broadcasted_iota(jnp.int32, s.shape, 0)
        col = lax.broadcasted_iota(jnp.int32, s.shape, 1)
        qrow = jnp.where(row >= tq, row - tq, row)
        update(jnp.where(col <= qrow, s, NEG_BIG))
        lam = _lam_from(lamvec_ref, lam_init)
        o = acc_ref[...] / l_ref[...]
        dlt = o[0:tq, :] - lam * o[tq:2 * tq, :]
        o_ref[...] = (_rms(dlt) * gsub_ref[...] * (1.0 - lam_init)).astype(o_ref.dtype)


def _attp(q, kb, vb, lamvec, g_sub, *, n_seq, seq, n_heads, lam_init):
    tq = 512 if seq % 512 == 0 else seq
    nq = seq // tq
    pairs = [(a, b) for a in range(nq) for b in range(a + 1)]
    qi_tab = jnp.array([a for a, _ in pairs], jnp.int32)
    ki_tab = jnp.array([b for _, b in pairs], jnp.int32)
    n_tok = q.shape[0]
    grid_spec = pltpu.PrefetchScalarGridSpec(
        num_scalar_prefetch=2,
        grid=(n_seq, n_heads, len(pairs)),
        in_specs=[
            pl.BlockSpec((4, LANES), lambda b, h, p, qt, kt: (0, 0)),
            pl.BlockSpec((tq, LANES), lambda b, h, p, qt, kt: (b * nq + qt[p], h)),
            pl.BlockSpec((tq, LANES), lambda b, h, p, qt, kt: (b * nq + kt[p], h)),
            pl.BlockSpec((tq, LANES), lambda b, h, p, qt, kt: (b * nq + kt[p], h)),
            pl.BlockSpec((1, LANES), lambda b, h, p, qt, kt: (0, 0)),
        ],
        out_specs=pl.BlockSpec((tq, LANES), lambda b, h, p, qt, kt: (b * nq + qt[p], h)),
        scratch_shapes=[pltpu.VMEM((2 * tq, LANES), BF16),
                        pltpu.VMEM((2 * tq, LANES), F32),
                        pltpu.VMEM((2 * tq, LANES), F32),
                        pltpu.VMEM((2 * tq, LANES), F32)],
    )
    return pl.pallas_call(
        functools.partial(_attp_kernel, tq=tq, lam_init=lam_init),
        grid_spec=grid_spec,
        out_shape=jax.ShapeDtypeStruct((n_tok, n_heads * LANES), BF16),
        compiler_params=pltpu.CompilerParams(
            dimension_semantics=("arbitrary", "arbitrary", "arbitrary"),
            vmem_limit_bytes=VMEM_LIMIT),
        name="attp",
    )(qi_tab, ki_tab, lamvec, q, kb, vb, g_sub.reshape(1, LANES))


def _atts_kernel(*refs, n_heads, T, P, rows_page, n_new, lam_init):
    pt_ref, lamvec_ref, q_ref, kn_ref, vn_ref, gsub_ref = refs[:6]
    k_refs = refs[6:6 + P]
    v_refs = refs[6 + P:6 + 2 * P]
    o_ref, qall_ref, m_ref, l_ref, acc_ref = refs[6 + 2 * P:]
    c = pl.program_id(1)
    half = LANES // 2
    R = n_heads * 2 * T

    @pl.when(c == 0)
    def _():
        q = q_ref[...].astype(F32)
        lane = lax.broadcasted_iota(jnp.int32, (T, LANES), 1)
        for hh in range(n_heads):
            qh = q[:, hh * LANES:(hh + 1) * LANES]
            zero = jnp.zeros_like(qh)
            qall_ref[(2 * hh) * T:(2 * hh + 1) * T, :] = jnp.where(lane < half, qh, zero)
            qall_ref[(2 * hh + 1) * T:(2 * hh + 2) * T, :] = jnp.where(lane >= half, qh, zero)
        m_ref[...] = jnp.full_like(m_ref, NEG_BIG)
        l_ref[...] = jnp.zeros_like(l_ref)
        acc_ref[...] = jnp.zeros_like(acc_ref)

    qall = qall_ref[...].astype(BF16)

    def update(s_list, v_list):
        m_prev = m_ref[...]
        m_cur = s_list[0].max(axis=1, keepdims=True)
        for s in s_list[1:]:
            m_cur = jnp.maximum(m_cur, s.max(axis=1, keepdims=True))
        m_new = jnp.maximum(m_prev, m_cur)
        alpha = jnp.exp(m_prev - m_new)
        l_new = alpha * l_ref[...]
        acc = alpha * acc_ref[...]
        for s, vv in zip(s_list, v_list):
            p = jnp.exp(s - pltpu.repeat(m_new, s.shape[1] // LANES, axis=1))
            l_new = l_new + jnp.sum(p, axis=1, keepdims=True)
            acc = acc + _dot(p.astype(BF16), vv)
        l_ref[...] = l_new
        acc_ref[...] = acc
        m_ref[...] = m_new

    row = lax.broadcasted_iota(jnp.int32, (R, rows_page), 0)
    col = lax.broadcasted_iota(jnp.int32, (R, rows_page), 1)
    head_ok = (col % n_heads) == (row // (2 * T))
    s_list, v_list = [], []
    for jj in range(P):
        s = _dot_nt(qall, k_refs[jj][...].astype(BF16))
        s_list.append(jnp.where(head_ok, s, NEG_BIG))
        v_list.append(v_refs[jj][...].astype(BF16))
    update(s_list, v_list)

    @pl.when(c == pl.num_programs(1) - 1)
    def _():
        nrow = lax.broadcasted_iota(jnp.int32, (R, n_new), 0)
        ncol = lax.broadcasted_iota(jnp.int32, (R, n_new), 1)
        ok = ((ncol % n_heads) == (nrow // (2 * T))) & ((ncol // n_heads) <= (nrow % T)) \
            & (ncol < T * n_heads)
        s = _dot_nt(qall, kn_ref[...])
        update([jnp.where(ok, s, NEG_BIG)], [vn_ref[...]])
        lam = _lam_from(lamvec_ref, lam_init)
        o = acc_ref[...] / l_ref[...]
        for hh in range(n_heads):
            dlt = o[(2 * hh) * T:(2 * hh + 1) * T, :] - lam * o[(2 * hh + 1) * T:(2 * hh + 2) * T, :]
            o_ref[:, hh * LANES:(hh + 1) * LANES] = _rms(dlt) * gsub_ref[...] * (1.0 - lam_init)


def _atts(q3, kn, vn, cache_k, cache_v, page_table, layer_base, lamvec, g_sub,
          *, n_heads, lam_init):
    bd, T, qw = q3.shape
    n_pages = page_table.shape[1]
    rows_page = cache_k.shape[1]
    n_new = kn.shape[1]
    P = 8 if n_pages % 8 == 0 else 1
    nc = n_pages // P

    def page_spec(jj):
        return pl.BlockSpec((None, rows_page, LANES),
                            lambda b, c, pt: (layer_base + pt[b, c * P + jj], 0, 0))

    in_specs = [
        pl.BlockSpec((4, LANES), lambda b, c, pt: (0, 0)),
        pl.BlockSpec((None, T, qw), lambda b, c, pt: (b, 0, 0)),
        pl.BlockSpec((None, n_new, LANES), lambda b, c, pt: (b, 0, 0)),
        pl.BlockSpec((None, n_new, LANES), lambda b, c, pt: (b, 0, 0)),
        pl.BlockSpec((1, LANES), lambda b, c, pt: (0, 0)),
    ] + [page_spec(jj) for jj in range(P)] + [page_spec(jj) for jj in range(P)]
    R = n_heads * 2 * T
    grid_spec = pltpu.PrefetchScalarGridSpec(
        num_scalar_prefetch=1, grid=(bd, nc), in_specs=in_specs,
        out_specs=pl.BlockSpec((None, T, qw), lambda b, c, pt: (b, 0, 0)),
        scratch_shapes=[pltpu.VMEM((R, LANES), F32), pltpu.VMEM((R, LANES), F32),
                        pltpu.VMEM((R, LANES), F32), pltpu.VMEM((R, LANES), F32)],
    )
    kern = functools.partial(_atts_kernel, n_heads=n_heads, T=T, P=P, rows_page=rows_page,
                             n_new=n_new, lam_init=lam_init)
    return pl.pallas_call(
        kern, grid_spec=grid_spec,
        out_shape=jax.ShapeDtypeStruct((bd, T, qw), F32),
        compiler_params=pltpu.CompilerParams(dimension_semantics=("arbitrary", "arbitrary"),
                                             vmem_limit_bytes=VMEM_LIMIT),
        name="atts",
    )(page_table, lamvec, q3, kn, vn, g_sub.reshape(1, LANES), *([cache_k] * P), *([cache_v] * P))


def _post_kernel(*refs, nseq, T, fc, has_prefix):
    if has_prefix:
        (x_ref, oa_ref, op_ref, mod_ref, gffn_ref, wout_ref, wup_ref, cw_ref, cb_ref, wdn_ref,
         cpre_ref, y_ref, cst_ref, h2_ref, acc_ref) = refs
    else:
        (x_ref, oa_ref, op_ref, mod_ref, gffn_ref, wout_ref, wup_ref, cw_ref, cb_ref, wdn_ref,
         y_ref, cst_ref, h2_ref, acc_ref) = refs
        cpre_ref = cst_ref
    j = pl.program_id(1)
    M = nseq * T
    d = x_ref.shape[-1]
    ff = wdn_ref.shape[0]
    att_w = oa_ref.shape[1]
    C = CONV_PREFIX_ROWS

    mix = _dot(oa_ref[...].astype(BF16), wout_ref[0:att_w, :]) \
        + _dot(op_ref[...].astype(BF16), wout_ref[att_w:, :])
    x1 = x_ref[...] + mod_ref[:, 2:3, :] * mix.reshape(nseq, T, d)
    y_ref[...] = x1
    h2 = _rms(x1) * gffn_ref[...] * (1.0 + mod_ref[:, 4:5, :]) + mod_ref[:, 3:4, :]
    h2_ref[...] = h2.reshape(M, d).astype(BF16)
    acc_ref[...] = jnp.zeros_like(acc_ref)

    if not has_prefix:
        @pl.when(j == 0)
        def _():
            cst_ref[...] = jnp.zeros_like(cst_ref)

    def conv(a, col):
        pre = cpre_ref[:, :, pl.ds(col, fc)]
        ext3 = jnp.concatenate([pre, a.reshape(nseq, T, fc)], axis=1)
        cst_ref[:, :, pl.ds(col, fc)] = ext3[:, T:T + C, :]
        ext = ext3.reshape(nseq * (C + T), fc)
        y = (cb_ref[:, pl.ds(col, fc)]
             + pltpu.roll(ext, 2, axis=0) * cw_ref[0:1, pl.ds(col, fc)]
             + pltpu.roll(ext, 1, axis=0) * cw_ref[1:2, pl.ds(col, fc)]
             + ext * cw_ref[2:3, pl.ds(col, fc)])
        return y.reshape(nseq, C + T, fc)[:, C:, :].reshape(M, fc)

    def chunk(ci, carry):
        cg = pl.multiple_of(ci * fc, fc)
        cu = pl.multiple_of(ff + ci * fc, fc)
        hb = h2_ref[...]
        g = conv(_dot(hb, wup_ref[:, pl.ds(cg, fc)]), cg)
        up = conv(_dot(hb, wup_ref[:, pl.ds(cu, fc)]), cu)
        hm = (_silu(g) * up).astype(BF16)
        acc_ref[...] += _dot(hm, wdn_ref[pl.ds(cg, fc), :])
        return carry

    lax.fori_loop(0, ff // fc, chunk, 0)
    y_ref[...] = y_ref[...] + mod_ref[:, 5:6, :] * acc_ref[...].reshape(nseq, T, d)


def _post(x, oa, op, mod, g_ffn, w_out, w_up, conv_w, conv_b, w_down, cprefix, *, nseq, T):
    ns_total, t_total, d = x.shape
    ff = w_down.shape[0]
    fc = 256 if ff % 256 == 0 else ff
    ni, nj = ns_total // nseq, t_total // T
    M = nseq * T
    has_prefix = cprefix is not None
    assert has_prefix or nseq == 1
    C = CONV_PREFIX_ROWS

    const = lambda shape: pl.BlockSpec(shape, lambda i, j: (0,) * len(shape),
                                       pipeline_mode=pl.Buffered(1))
    tok = lambda w: pl.BlockSpec((M, w), lambda i, j: (i * nj + j, 0))
    in_specs = [
        pl.BlockSpec((nseq, T, d), lambda i, j: (i, j, 0)),
        tok(oa.shape[1]), tok(op.shape[1]),
        pl.BlockSpec((nseq, 6, d), lambda i, j: (i, 0, 0)),
        const((1, d)), const(w_out.shape), const(w_up.shape), const((CONV_W, 2 * ff)),
        const((1, 2 * ff)), const(w_down.shape),
    ]
    args = [x, oa, op, mod, g_ffn.reshape(1, d), w_out, w_up, conv_w, conv_b.reshape(1, 2 * ff), w_down]
    if has_prefix:
        in_specs.append(pl.BlockSpec((nseq, C, 2 * ff), lambda i, j: (i, 0, 0)))
        args.append(cprefix)
    return pl.pallas_call(
        functools.partial(_post_kernel, nseq=nseq, T=T, fc=fc, has_prefix=has_prefix),
        grid=(ni, nj), in_specs=in_specs,
        out_specs=(pl.BlockSpec((nseq, T, d), lambda i, j: (i, j, 0)),
                   pl.BlockSpec((nseq, C, 2 * ff), lambda i, j: (i, 0, 0))),
        out_shape=(jax.ShapeDtypeStruct(x.shape, F32),
                   jax.ShapeDtypeStruct((ns_total, C, 2 * ff), F32)),
        scratch_shapes=[pltpu.VMEM((M, d), BF16), pltpu.VMEM((M, d), F32)],
        compiler_params=pltpu.CompilerParams(dimension_semantics=("arbitrary", "arbitrary"),
                                             vmem_limit_bytes=VMEM_LIMIT),
        name="post",
    )(*args)


def _rope_tables(pos, d_qk, rot_dim):
    half = rot_dim // 2
    inv = ROPE_THETA ** (-jnp.arange(0, rot_dim, 2, dtype=F32) / rot_dim)
    ang = pos.astype(F32)[:, None] * inv[None, :]
    cos, sin = jnp.cos(ang), jnp.sin(ang)
    l64 = jnp.arange(LANES) % d_qk
    idx = l64 % half
    cos_t = jnp.where(l64 < rot_dim, cos[:, idx], 1.0)
    sa = jnp.where(l64 < half, -sin[:, idx], 0.0)
    sb = jnp.where((l64 >= half) & (l64 < rot_dim), sin[:, idx], 0.0)
    return cos_t, sa, sb


def _tile_rows(n, target):
    t = min(n, target)
    while n % t:
        t //= 2
    return t


def kernel(x_prompt, x_sample, cache_k, cache_v, state_pool, state_conv, page_table, c_prompt, c_sample, w_ada, b_ada, g_attn, w_in, g_q, g_k, lam_q1, lam_k1, lam_q2, lam_k2, g_sub, w_pool, pool_scale, w_out, g_ffn, w_up, conv_w, conv_b, w_down):
    B, S, D = x_prompt.shape
    Bd, T, _ = x_sample.shape
    depth, n_pool, page_size, n_heads, dk2 = cache_k.shape
    d_qk = g_q.shape[-1]
    assert dk2 == 2 * d_qk == LANES and cache_v.shape[-1] == LANES
    rot_dim = d_qk // 4
    assert rot_dim == 16
    past_len = page_table.shape[1] * page_size
    pool_w = pool_scale.shape[-1]
    ff = w_down.shape[1]
    q_w = n_heads * LANES

    ck = cache_k.reshape(depth * n_pool, page_size * n_heads, LANES)
    cv = cache_v.reshape(depth * n_pool, page_size * n_heads, LANES)

    cos_p, sa_p, sb_p = _rope_tables(jnp.arange(S), d_qk, rot_dim)
    cos_s, sa_s, sb_s = _rope_tables(past_len + jnp.arange(T), d_qk, rot_dim)
    blk = jnp.arange(q_w) // d_qk
    gmat = jnp.where(blk[:, None] == blk[None, :], 1.0 / d_qk, 0.0).astype(BF16)

    tp = _tile_rows(S, 512)
    ns = _tile_rows(Bd, max(1, 512 // T))
    ns_post = _tile_rows(Bd, 16)
    n_new = max(LANES, T * n_heads)

    yp, ys = x_prompt, x_sample
    outs = [[] for _ in range(8)]
    for l in range(depth):
        lam_init = 0.8 - 0.6 * math.exp(-0.3 * l)
        lamvec = jnp.pad(jnp.stack([lam_q1[l], lam_k1[l], lam_q2[l], lam_k2[l]]).astype(F32),
                         ((0, 0), (0, LANES - d_qk)))
        w_in_b, w_out_b = w_in[l].astype(BF16), w_out[l].astype(BF16)
        w_up_b, w_dn_b, w_pool_b = w_up[l].astype(BF16), w_down[l].astype(BF16), w_pool[l].astype(BF16)
        gq_t = jnp.tile(g_q[l], q_w // d_qk).reshape(1, q_w)
        gk_t = jnp.tile(g_k[l], q_w // d_qk).reshape(1, q_w)

        mod = _ada(jnp.concatenate([c_prompt, c_sample], axis=0), w_ada[l], b_ada[l])
        mod_p = mod[:B].reshape(B, 6, D)
        mod_s = mod[B:].reshape(Bd, 6, D)

        q, k, kb, v, vb, opool, pst = _pre(
            yp, mod_p, g_attn[l], w_in_b, gq_t, gk_t, gmat, cos_p, sa_p, sb_p, w_pool_b,
            pool_scale[l], None, nseq=1, T=tp, n_heads=n_heads, pos_base=0)
        oatt = _attp(q, kb, vb, lamvec, g_sub[l], n_seq=B, seq=S, n_heads=n_heads, lam_init=lam_init)
        yp, cst = _post(yp, oatt, opool, mod_p, g_ffn[l], w_out_b, w_up_b, conv_w[l], conv_b[l],
                        w_dn_b, None, nseq=1, T=tp)
        outs[0].append(k.reshape(B, S, n_heads, LANES))
        outs[1].append(v.reshape(B, S, n_heads, LANES))
        outs[2].append(pst[:, 1:, :])
        outs[3].append(cst[:, CONV_PREFIX_ROWS - (CONV_W - 1):, :])

        ppre = jnp.pad(state_pool[l], ((0, 0), (POOL_PREFIX_ROWS - state_pool.shape[2], 0), (0, 0)))
        cpre = jnp.pad(state_conv[l], ((0, 0), (CONV_PREFIX_ROWS - state_conv.shape[2], 0), (0, 0)))
        q, k, kb, v, vb, opool, pst = _pre(
            ys, mod_s, g_attn[l], w_in_b, gq_t, gk_t, gmat, cos_s, sa_s, sb_s, w_pool_b,
            pool_scale[l], ppre, nseq=ns, T=T, n_heads=n_heads, pos_base=past_len)
        kn = jnp.pad(kb.reshape(Bd, T * n_heads, LANES), ((0, 0), (0, n_new - T * n_heads), (0, 0)))
        vn = jnp.pad(vb.reshape(Bd, T * n_heads, LANES), ((0, 0), (0, n_new - T * n_heads), (0, 0)))
        oatt = _atts(q.astype(F32).reshape(Bd, T, q_w), kn, vn, ck, cv, page_table, l * n_pool,
                     lamvec, g_sub[l], n_heads=n_heads, lam_init=lam_init)
        ys, cst = _post(ys, oatt.reshape(Bd * T, q_w), opool, mod_s, g_ffn[l], w_out_b, w_up_b,
                        conv_w[l], conv_b[l], w_dn_b, cpre, nseq=ns_post, T=T)
        outs[4].append(k.reshape(Bd, T, n_heads, LANES))
        outs[5].append(v.reshape(Bd, T, n_heads, LANES))
        outs[6].append(pst[:, 1:, :])
        outs[7].append(cst[:, CONV_PREFIX_ROWS - (CONV_W - 1):, :])

    return (yp, ys) + tuple(jnp.stack(o) for o in outs)
```

```python
import functools
import math

import jax
import jax.numpy as jnp
from jax import lax
from jax.experimental import pallas as pl
from jax.experimental.pallas import tpu as pltpu

F32 = jnp.float32
BF16 = jnp.bfloat16

POOL_WINDOWS = (2, 4, 8, 16)
POOL_PREFIX_ROWS = 16
CONV_W = 3
CONV_PREFIX_ROWS = 8
ROPE_THETA = 500000.0
EPS = 1e-6
LANES = 128
NEG_BIG = -1e30
VMEM_LIMIT = 56 * 1024 * 1024


def _silu(x):
    return x / (1.0 + jnp.exp(-x))


def _rms(x, axis=-1):
    return x * lax.rsqrt(jnp.mean(x * x, axis=axis, keepdims=True) + EPS)


def _dot(a, b):
    return jnp.dot(a, b, preferred_element_type=F32)


def _dot_nt(a, b):
    return lax.dot_general(a, b, (((1,), (1,)), ((), ())), preferred_element_type=F32)


def _split_dot(x, w_bf16):
    hi = x.astype(BF16)
    lo = (x - hi.astype(F32)).astype(BF16)
    return _dot(hi, w_bf16) + _dot(lo, w_bf16)


def _ada_kernel(c_ref, w_ref, b_ref, o_ref):
    s = _silu(c_ref[...]).astype(BF16)
    o_ref[...] = _dot(s, w_ref[...].astype(BF16)) + b_ref[...]


def _ada(c, w_ada, b_ada):
    nb, d = c.shape
    n = w_ada.shape[1]
    bn = 1024 if n % 1024 == 0 else n
    return pl.pallas_call(
        _ada_kernel,
        grid=(n // bn,),
        in_specs=[pl.BlockSpec((nb, d), lambda j: (0, 0)),
                  pl.BlockSpec((d, bn), lambda j: (0, j)),
                  pl.BlockSpec((1, bn), lambda j: (0, j))],
        out_specs=pl.BlockSpec((nb, bn), lambda j: (0, j)),
        out_shape=jax.ShapeDtypeStruct((nb, n), F32),
        compiler_params=pltpu.CompilerParams(dimension_semantics=("arbitrary",),
                                             vmem_limit_bytes=VMEM_LIMIT),
        name="ada",
    )(c, w_ada, b_ada.reshape(1, n))


def _pre_kernel(*refs, nseq, T, n_heads, att_w, pool_w, pos_base, has_prefix):
    if has_prefix:
        (x_ref, mod_ref, gattn_ref, win_ref, gq_ref, gk_ref, gmat_ref, cos_ref, sa_ref, sb_ref,
         wpool_ref, pscale_ref, prefix_ref,
         q_ref, k_ref, kb_ref, v_ref, vb_ref, op_ref, pst_ref) = refs
    else:
        (x_ref, mod_ref, gattn_ref, win_ref, gq_ref, gk_ref, gmat_ref, cos_ref, sa_ref, sb_ref,
         wpool_ref, pscale_ref,
         q_ref, k_ref, kb_ref, v_ref, vb_ref, op_ref, pst_ref) = refs
    j = pl.program_id(1)
    M = nseq * T
    q_w = n_heads * LANES

    x = x_ref[...]
    d = x.shape[-1]
    h = _rms(x) * gattn_ref[...] * (1.0 + mod_ref[:, 1:2, :]) + mod_ref[:, 0:1, :]
    p = _dot(h.reshape(M, d).astype(BF16), win_ref[...])

    cos = jnp.broadcast_to(cos_ref[...][None], (nseq, T, LANES)).reshape(M, LANES)
    sa = jnp.broadcast_to(sa_ref[...][None], (nseq, T, LANES)).reshape(M, LANES)
    sb = jnp.broadcast_to(sb_ref[...][None], (nseq, T, LANES)).reshape(M, LANES)

    def qk_norm_rope(t, g):
        ms = _split_dot(t * t, gmat_ref[...])
        t = t * lax.rsqrt(ms + EPS) * g
        outs = []
        for hh in range(n_heads):
            th = t[:, hh * LANES:(hh + 1) * LANES]
            half = 8
            outs.append(th * cos + pltpu.roll(th, LANES - half, axis=1) * sa
                        + pltpu.roll(th, half, axis=1) * sb)
        return jnp.concatenate(outs, axis=1)

    q = qk_norm_rope(p[:, 0:q_w], gq_ref[...])
    k = qk_norm_rope(p[:, q_w:2 * q_w], gk_ref[...])
    v = p[:, 2 * q_w:2 * q_w + att_w]
    u = p[:, 2 * q_w + att_w:2 * q_w + att_w + pool_w]

    scale = (LANES // 2) ** -0.5
    q_ref[...] = (q * scale).astype(BF16)
    kb_ref[...] = k.astype(BF16)
    vb_ref[...] = v.astype(BF16)
    for hh in range(n_heads):
        k_ref[pl.ds(hh, M, stride=n_heads), :] = k[:, hh * LANES:(hh + 1) * LANES]
        v_ref[pl.ds(hh, M, stride=n_heads), :] = v[:, hh * LANES:(hh + 1) * LANES]

    P = POOL_PREFIX_ROWS
    if has_prefix:
        pre = prefix_ref[...]
    else:
        @pl.when(j == 0)
        def _():
            pst_ref[...] = jnp.zeros_like(pst_ref)
        pre = pst_ref[...]
    ext3 = jnp.concatenate([pre, u.reshape(nseq, T, pool_w)], axis=1)
    pst_ref[...] = ext3[:, T:T + P, :]
    ext = ext3.reshape(nseq * (P + T), pool_w)

    pos = pos_base + j * T + lax.broadcasted_iota(jnp.int32, (nseq, T, LANES), 1)
    g_w = pool_w // len(POOL_WINDOWS)
    outs = []
    for g, w in enumerate(POOL_WINDOWS):
        e = ext[:, g * g_w:(g + 1) * g_w]
        s = e
        sh = 1
        while sh < w:
            s = s + pltpu.roll(s, sh, axis=0)
            sh *= 2
        s3 = s.reshape(nseq, P + T, g_w)[:, P:, :]
        e3 = e.reshape(nseq, P + T, g_w)[:, P:, :]
        cnt = jnp.minimum(w, pos + 1).astype(F32)
        pooled = (s3 / cnt - e3).reshape(M, g_w).astype(BF16)
        outs.append(_dot(pooled, wpool_ref[g]))
    op_ref[...] = (jnp.concatenate(outs, axis=1) * pscale_ref[...]).astype(BF16)


def _pre(x, mod, g_attn, w_in, gq_t, gk_t, gmat, cos, sa, sb, w_pool, pool_scale, prefix,
         *, nseq, T, n_heads, pos_base):
    ns_total, t_total, d = x.shape
    in_w = w_in.shape[1]
    pool_w = pool_scale.shape[-1]
    q_w = n_heads * LANES
    att_w = in_w - 2 * q_w - pool_w
    assert pool_w // len(POOL_WINDOWS) == LANES and att_w == q_w
    ni, nj = ns_total // nseq, t_total // T
    M = nseq * T
    n_tok = ns_total * t_total
    has_prefix = prefix is not None
    assert has_prefix or nseq == 1

    const = lambda shape: pl.BlockSpec(shape, lambda i, j: (0,) * len(shape),
                                       pipeline_mode=pl.Buffered(1))
    tok = lambda w: pl.BlockSpec((M, w), lambda i, j: (i * nj + j, 0))
    in_specs = [
        pl.BlockSpec((nseq, T, d), lambda i, j: (i, j, 0)),
        pl.BlockSpec((nseq, 6, d), lambda i, j: (i, 0, 0)),
        const((1, d)), const((d, in_w)), const((1, q_w)), const((1, q_w)), const((q_w, q_w)),
        pl.BlockSpec((T, LANES), lambda i, j: (j, 0)),
        pl.BlockSpec((T, LANES), lambda i, j: (j, 0)),
        pl.BlockSpec((T, LANES), lambda i, j: (j, 0)),
        const(w_pool.shape), const((1, pool_w)),
    ]
    args = [x, mod, g_attn.reshape(1, d), w_in, gq_t, gk_t, gmat, cos, sa, sb,
            w_pool, pool_scale.reshape(1, pool_w)]
    if has_prefix:
        in_specs.append(pl.BlockSpec((nseq, POOL_PREFIX_ROWS, pool_w), lambda i, j: (i, 0, 0)))
        args.append(prefix)
    out_shape = (
        jax.ShapeDtypeStruct((n_tok, q_w), BF16),
        jax.ShapeDtypeStruct((n_tok * n_heads, LANES), F32),
        jax.ShapeDtypeStruct((n_tok, q_w), BF16),
        jax.ShapeDtypeStruct((n_tok * n_heads, LANES), F32),
        jax.ShapeDtypeStruct((n_tok, att_w), BF16),
        jax.ShapeDtypeStruct((n_tok, pool_w), BF16),
        jax.ShapeDtypeStruct((ns_total, POOL_PREFIX_ROWS, pool_w), F32),
    )
    tok_head = pl.BlockSpec((M * n_heads, LANES), lambda i, j: (i * nj + j, 0))
    out_specs = (tok(q_w), tok_head, tok(q_w), tok_head, tok(att_w), tok(pool_w),
                 pl.BlockSpec((nseq, POOL_PREFIX_ROWS, pool_w), lambda i, j: (i, 0, 0)))
    kern = functools.partial(_pre_kernel, nseq=nseq, T=T, n_heads=n_heads, att_w=att_w,
                             pool_w=pool_w, pos_base=pos_base, has_prefix=has_prefix)
    return pl.pallas_call(
        kern, grid=(ni, nj), in_specs=in_specs, out_specs=out_specs, out_shape=out_shape,
        compiler_params=pltpu.CompilerParams(dimension_semantics=("arbitrary", "arbitrary"),
                                             vmem_limit_bytes=VMEM_LIMIT),
        name="pre",
    )(*args)


def _lam_from(lamvec_ref, lam_init):
    a = jnp.sum(lamvec_ref[0:1, :] * lamvec_ref[1:2, :], axis=-1, keepdims=True)
    b = jnp.sum(lamvec_ref[2:3, :] * lamvec_ref[3:4, :], axis=-1, keepdims=True)
    return jnp.exp(a) - jnp.exp(b) + lam_init


def _attp_kernel(lamvec_ref, q_ref, k_ref, v_ref, gsub_ref, o_ref, qs_ref, m_ref, l_ref, acc_ref,
                 *, tq, nq, lam_init):
    half = LANES // 2
    lane = lax.broadcasted_iota(jnp.int32, (tq, LANES), 1)
    for qi in range(nq):
        q = q_ref[qi * tq:(qi + 1) * tq, :]
        zero = jnp.zeros_like(q)
        qs_ref[qi, 0:tq, :] = jnp.where(lane < half, q, zero)
        qs_ref[qi, tq:2 * tq, :] = jnp.where(lane >= half, q, zero)
    m_ref[...] = jnp.full_like(m_ref, NEG_BIG)
    l_ref[...] = jnp.zeros_like(l_ref)
    acc_ref[...] = jnp.zeros_like(acc_ref)

    row = lax.broadcasted_iota(jnp.int32, (2 * tq, tq), 0)
    col = lax.broadcasted_iota(jnp.int32, (2 * tq, tq), 1)
    causal = col <= jnp.where(row >= tq, row - tq, row)

    def update(qi, ki):
        s = _dot_nt(qs_ref[qi], k_ref[ki * tq:(ki + 1) * tq, :])
        if qi == ki:
            s = jnp.where(causal, s, NEG_BIG)
        m_prev = m_ref[qi]
        m_new = jnp.maximum(m_prev, jnp.max(s, axis=1, keepdims=True))
        alpha = jnp.exp(m_prev - m_new)
        p = jnp.exp(s - pltpu.repeat(m_new, tq // LANES, axis=1))
        l_ref[qi] = alpha * l_ref[qi] + jnp.sum(p, axis=1, keepdims=True)
        acc_ref[qi] = alpha * acc_ref[qi] + _dot(p.astype(BF16), v_ref[ki * tq:(ki + 1) * tq, :])
        m_ref[qi] = m_new

    for ki in range(nq):
        for qi in range(ki, nq):
            update(qi, ki)

    lam = _lam_from(lamvec_ref, lam_init)
    for qi in range(nq):
        o = acc_ref[qi] / l_ref[qi]
        dlt = o[0:tq, :] - lam * o[tq:2 * tq, :]
        o_ref[qi * tq:(qi + 1) * tq, :] = (_rms(dlt) * gsub_ref[...]
                                           * (1.0 - lam_init)).astype(o_ref.dtype)


def _attp(q, kb, vb, lamvec, g_sub, *, n_seq, seq, n_heads, lam_init):
    tq = 512 if seq % 512 == 0 else seq
    nq = seq // tq
    n_tok = q.shape[0]
    seq_head = pl.BlockSpec((seq, LANES), lambda b, h: (b, h))
    return pl.pallas_call(
        functools.partial(_attp_kernel, tq=tq, nq=nq, lam_init=lam_init),
        grid=(n_seq, n_heads),
        in_specs=[pl.BlockSpec((4, LANES), lambda b, h: (0, 0)),
                  seq_head, seq_head, seq_head,
                  pl.BlockSpec((1, LANES), lambda b, h: (0, 0))],
        out_specs=seq_head,
        out_shape=jax.ShapeDtypeStruct((n_tok, n_heads * LANES), BF16),
        scratch_shapes=[pltpu.VMEM((nq, 2 * tq, LANES), BF16),
                        pltpu.VMEM((nq, 2 * tq, LANES), F32),
                        pltpu.VMEM((nq, 2 * tq, LANES), F32),
                        pltpu.VMEM((nq, 2 * tq, LANES), F32)],
        compiler_params=pltpu.CompilerParams(dimension_semantics=("arbitrary", "arbitrary"),
                                             vmem_limit_bytes=VMEM_LIMIT),
        name="attp",
    )(lamvec, q, kb, vb, g_sub.reshape(1, LANES))


def _atts_kernel(*refs, n_heads, T, P, rows_page, n_new, lam_init):
    pt_ref, lamvec_ref, q_ref, kn_ref, vn_ref, gsub_ref = refs[:6]
    k_refs = refs[6:6 + P]
    v_refs = refs[6 + P:6 + 2 * P]
    o_ref, qall_ref, m_ref, l_ref, acc_ref = refs[6 + 2 * P:]
    c = pl.program_id(1)
    half = LANES // 2
    R = n_heads * 2 * T

    @pl.when(c == 0)
    def _():
        q = q_ref[...].astype(F32)
        lane = lax.broadcasted_iota(jnp.int32, (T, LANES), 1)
        for hh in range(n_heads):
            qh = q[:, hh * LANES:(hh + 1) * LANES]
            zero = jnp.zeros_like(qh)
            qall_ref[(2 * hh) * T:(2 * hh + 1) * T, :] = jnp.where(lane < half, qh, zero)
            qall_ref[(2 * hh + 1) * T:(2 * hh + 2) * T, :] = jnp.where(lane >= half, qh, zero)
        m_ref[...] = jnp.full_like(m_ref, NEG_BIG)
        l_ref[...] = jnp.zeros_like(l_ref)
        acc_ref[...] = jnp.zeros_like(acc_ref)

    qall = qall_ref[...].astype(BF16)

    def update(s_list, v_list):
        parts = []
        for s, vv in zip(s_list, v_list):
            mj = jnp.broadcast_to(jnp.max(s, axis=1, keepdims=True), (R, LANES))
            p = jnp.exp(s - pltpu.repeat(mj, s.shape[1] // LANES, axis=1))
            parts.append((mj, jnp.sum(p, axis=1, keepdims=True), _dot(p.astype(BF16), vv)))
        m_prev = m_ref[...]
        m_new = m_prev
        for mj, _, _ in parts:
            m_new = jnp.maximum(m_new, mj)
        alpha = jnp.exp(m_prev - m_new)
        l_new = alpha * l_ref[...]
        acc = alpha * acc_ref[...]
        for mj, lj, oj in parts:
            wj = jnp.exp(mj - m_new)
            l_new = l_new + wj * lj
            acc = acc + wj * oj
        l_ref[...] = l_new
        acc_ref[...] = acc
        m_ref[...] = m_new

    row = lax.broadcasted_iota(jnp.int32, (R, rows_page), 0)
    col = lax.broadcasted_iota(jnp.int32, (R, rows_page), 1)
    head_ok = (col % n_heads) == (row // (2 * T))
    s_list, v_list = [], []
    for jj in range(P):
        s = _dot_nt(qall, k_refs[jj][...].astype(BF16))
        s_list.append(jnp.where(head_ok, s, NEG_BIG))
        v_list.append(v_refs[jj][...].astype(BF16))
    update(s_list, v_list)

    @pl.when(c == pl.num_programs(1) - 1)
    def _():
        nrow = lax.broadcasted_iota(jnp.int32, (R, n_new), 0)
        ncol = lax.broadcasted_iota(jnp.int32, (R, n_new), 1)
        ok = ((ncol % n_heads) == (nrow // (2 * T))) & ((ncol // n_heads) <= (nrow % T)) \
            & (ncol < T * n_heads)
        s = _dot_nt(qall, kn_ref[...])
        update([jnp.where(ok, s, NEG_BIG)], [vn_ref[...]])
        lam = _lam_from(lamvec_ref, lam_init)
        o = acc_ref[...] / l_ref[...]
        for hh in range(n_heads):
            dlt = o[(2 * hh) * T:(2 * hh + 1) * T, :] - lam * o[(2 * hh + 1) * T:(2 * hh + 2) * T, :]
            o_ref[:, hh * LANES:(hh + 1) * LANES] = _rms(dlt) * gsub_ref[...] * (1.0 - lam_init)


def _atts(q3, kn, vn, cache_k, cache_v, page_table, layer_base, lamvec, g_sub,
          *, n_heads, lam_init):
    bd, T, qw = q3.shape
    n_pages = page_table.shape[1]
    rows_page = cache_k.shape[1]
    n_new = kn.shape[1]
    P = 8 if n_pages % 8 == 0 else 1
    nc = n_pages // P

    def page_spec(jj):
        return pl.BlockSpec((None, rows_page, LANES),
                            lambda b, c, pt: (layer_base + pt[b, c * P + jj], 0, 0))

    in_specs = [
        pl.BlockSpec((4, LANES), lambda b, c, pt: (0, 0)),
        pl.BlockSpec((None, T, qw), lambda b, c, pt: (b, 0, 0)),
        pl.BlockSpec((None, n_new, LANES), lambda b, c, pt: (b, 0, 0)),
        pl.BlockSpec((None, n_new, LANES), lambda b, c, pt: (b, 0, 0)),
        pl.BlockSpec((1, LANES), lambda b, c, pt: (0, 0)),
    ] + [page_spec(jj) for jj in range(P)] + [page_spec(jj) for jj in range(P)]
    R = n_heads * 2 * T
    grid_spec = pltpu.PrefetchScalarGridSpec(
        num_scalar_prefetch=1, grid=(bd, nc), in_specs=in_specs,
        out_specs=pl.BlockSpec((None, T, qw), lambda b, c, pt: (b, 0, 0)),
        scratch_shapes=[pltpu.VMEM((R, LANES), F32), pltpu.VMEM((R, LANES), F32),
                        pltpu.VMEM((R, LANES), F32), pltpu.VMEM((R, LANES), F32)],
    )
    kern = functools.partial(_atts_kernel, n_heads=n_heads, T=T, P=P, rows_page=rows_page,
                             n_new=n_new, lam_init=lam_init)
    return pl.pallas_call(
        kern, grid_spec=grid_spec,
        out_shape=jax.ShapeDtypeStruct((bd, T, qw), F32),
        compiler_params=pltpu.CompilerParams(dimension_semantics=("arbitrary", "arbitrary"),
                                             vmem_limit_bytes=VMEM_LIMIT),
        name="atts",
    )(page_table, lamvec, q3, kn, vn, g_sub.reshape(1, LANES), *([cache_k] * P), *([cache_v] * P))


def _post_kernel(*refs, nseq, T, fc, has_prefix):
    if has_prefix:
        (x_ref, oa_ref, op_ref, mod_ref, gffn_ref, wout_ref, wup_ref, cw_ref, cb_ref, wdn_ref,
         cpre_ref, y_ref, cst_ref, h2_ref, a_ref, hm_ref) = refs
    else:
        (x_ref, oa_ref, op_ref, mod_ref, gffn_ref, wout_ref, wup_ref, cw_ref, cb_ref, wdn_ref,
         y_ref, cst_ref, h2_ref, a_ref, hm_ref) = refs
        cpre_ref = cst_ref
    j = pl.program_id(1)
    M = nseq * T
    d = x_ref.shape[-1]
    ff = wdn_ref.shape[0]
    att_w = oa_ref.shape[1]
    C = CONV_PREFIX_ROWS

    mix = _dot(oa_ref[...].astype(BF16), wout_ref[0:att_w, :]) \
        + _dot(op_ref[...].astype(BF16), wout_ref[att_w:, :])
    x1 = x_ref[...] + mod_ref[:, 2:3, :] * mix.reshape(nseq, T, d)
    y_ref[...] = x1
    h2 = _rms(x1) * gffn_ref[...] * (1.0 + mod_ref[:, 4:5, :]) + mod_ref[:, 3:4, :]
    h2_ref[...] = h2.reshape(M, d).astype(BF16)

    if not has_prefix:
        @pl.when(j == 0)
        def _():
            cst_ref[...] = jnp.zeros_like(cst_ref)

    def conv(a, col):
        pre = cpre_ref[:, :, col:col + fc]
        ext3 = jnp.concatenate([pre, a.reshape(nseq, T, fc)], axis=1)
        cst_ref[:, :, col:col + fc] = ext3[:, T:T + C, :]
        ext = ext3.reshape(nseq * (C + T), fc)
        y = (cb_ref[:, col:col + fc]
             + pltpu.roll(ext, 2, axis=0) * cw_ref[0:1, col:col + fc]
             + pltpu.roll(ext, 1, axis=0) * cw_ref[1:2, col:col + fc]
             + ext * cw_ref[2:3, col:col + fc])
        return y.reshape(nseq, C + T, fc)[:, C:, :].reshape(M, fc)

    nc = ff // fc

    def up_proj(ci):
        hb = h2_ref[...]
        a_ref[ci % 2, :, 0:fc] = _dot(hb, wup_ref[:, ci * fc:(ci + 1) * fc])
        a_ref[ci % 2, :, fc:2 * fc] = _dot(hb, wup_ref[:, ff + ci * fc:ff + (ci + 1) * fc])

    def conv_gate(ci):
        g = conv(a_ref[ci % 2, :, 0:fc], ci * fc)
        up = conv(a_ref[ci % 2, :, fc:2 * fc], ff + ci * fc)
        hm_ref[:, ci * fc:(ci + 1) * fc] = (_silu(g) * up).astype(BF16)

    up_proj(0)
    for ci in range(nc):
        if ci + 1 < nc:
            up_proj(ci + 1)
        conv_gate(ci)
    f = _dot(hm_ref[...], wdn_ref[...])
    y_ref[...] = y_ref[...] + mod_ref[:, 5:6, :] * f.reshape(nseq, T, d)


def _post(x, oa, op, mod, g_ffn, w_out, w_up, conv_w, conv_b, w_down, cprefix, *, nseq, T):
    ns_total, t_total, d = x.shape
    ff = w_down.shape[0]
    fc = 256 if ff % 256 == 0 else ff
    ni, nj = ns_total // nseq, t_total // T
    M = nseq * T
    has_prefix = cprefix is not None
    assert has_prefix or nseq == 1
    C = CONV_PREFIX_ROWS

    const = lambda shape: pl.BlockSpec(shape, lambda i, j: (0,) * len(shape),
                                       pipeline_mode=pl.Buffered(1))
    tok = lambda w: pl.BlockSpec((M, w), lambda i, j: (i * nj + j, 0))
    in_specs = [
        pl.BlockSpec((nseq, T, d), lambda i, j: (i, j, 0)),
        tok(oa.shape[1]), tok(op.shape[1]),
        pl.BlockSpec((nseq, 6, d), lambda i, j: (i, 0, 0)),
        const((1, d)), const(w_out.shape), const(w_up.shape), const((CONV_W, 2 * ff)),
        const((1, 2 * ff)), const(w_down.shape),
    ]
    args = [x, oa, op, mod, g_ffn.reshape(1, d), w_out, w_up, conv_w, conv_b.reshape(1, 2 * ff), w_down]
    if has_prefix:
        in_specs.append(pl.BlockSpec((nseq, C, 2 * ff), lambda i, j: (i, 0, 0)))
        args.append(cprefix)
    return pl.pallas_call(
        functools.partial(_post_kernel, nseq=nseq, T=T, fc=fc, has_prefix=has_prefix),
        grid=(ni, nj), in_specs=in_specs,
        out_specs=(pl.BlockSpec((nseq, T, d), lambda i, j: (i, j, 0)),
                   pl.BlockSpec((nseq, C, 2 * ff), lambda i, j: (i, 0, 0))),
        out_shape=(jax.ShapeDtypeStruct(x.shape, F32),
                   jax.ShapeDtypeStruct((ns_total, C, 2 * ff), F32)),
        scratch_shapes=[pltpu.VMEM((M, d), BF16),
                        pltpu.VMEM((2, M, 2 * fc), F32),
                        pltpu.VMEM((M, ff), BF16)],
        compiler_params=pltpu.CompilerParams(dimension_semantics=("arbitrary", "arbitrary"),
                                             vmem_limit_bytes=VMEM_LIMIT),
        name="post",
    )(*args)


def _rope_tables(pos, d_qk, rot_dim):
    half = rot_dim // 2
    inv = ROPE_THETA ** (-jnp.arange(0, rot_dim, 2, dtype=F32) / rot_dim)
    ang = pos.astype(F32)[:, None] * inv[None, :]
    cos, sin = jnp.cos(ang), jnp.sin(ang)
    l64 = jnp.arange(LANES) % d_qk
    idx = l64 % half
    cos_t = jnp.where(l64 < rot_dim, cos[:, idx], 1.0)
    sa = jnp.where(l64 < half, -sin[:, idx], 0.0)
    sb = jnp.where((l64 >= half) & (l64 < rot_dim), sin[:, idx], 0.0)
    return cos_t, sa, sb


def _tile_rows(n, target):
    t = min(n, target)
    while n % t:
        t //= 2
    return t


def kernel(x_prompt, x_sample, cache_k, cache_v, state_pool, state_conv, page_table, c_prompt, c_sample, w_ada, b_ada, g_attn, w_in, g_q, g_k, lam_q1, lam_k1, lam_q2, lam_k2, g_sub, w_pool, pool_scale, w_out, g_ffn, w_up, conv_w, conv_b, w_down):
    B, S, D = x_prompt.shape
    Bd, T, _ = x_sample.shape
    depth, n_pool, page_size, n_heads, dk2 = cache_k.shape
    d_qk = g_q.shape[-1]
    assert dk2 == 2 * d_qk == LANES and cache_v.shape[-1] == LANES
    rot_dim = d_qk // 4
    assert rot_dim == 16
    past_len = page_table.shape[1] * page_size
    pool_w = pool_scale.shape[-1]
    ff = w_down.shape[1]
    q_w = n_heads * LANES

    ck = cache_k.reshape(depth * n_pool, page_size * n_heads, LANES)
    cv = cache_v.reshape(depth * n_pool, page_size * n_heads, LANES)

    cos_p, sa_p, sb_p = _rope_tables(jnp.arange(S), d_qk, rot_dim)
    cos_s, sa_s, sb_s = _rope_tables(past_len + jnp.arange(T), d_qk, rot_dim)
    blk = jnp.arange(q_w) // d_qk
    gmat = jnp.where(blk[:, None] == blk[None, :], 1.0 / d_qk, 0.0).astype(BF16)

    tp = _tile_rows(S, 512)
    ns = _tile_rows(Bd, max(1, 512 // T))
    ns_post = _tile_rows(Bd, 16)
    n_new = max(LANES, T * n_heads)

    yp, ys = x_prompt, x_sample
    outs = [[] for _ in range(8)]
    for l in range(depth):
        lam_init = 0.8 - 0.6 * math.exp(-0.3 * l)
        lamvec = jnp.pad(jnp.stack([lam_q1[l], lam_k1[l], lam_q2[l], lam_k2[l]]).astype(F32),
                         ((0, 0), (0, LANES - d_qk)))
        w_in_b, w_out_b = w_in[l].astype(BF16), w_out[l].astype(BF16)
        w_up_b, w_dn_b, w_pool_b = w_up[l].astype(BF16), w_down[l].astype(BF16), w_pool[l].astype(BF16)
        gq_t = jnp.tile(g_q[l], q_w // d_qk).reshape(1, q_w)
        gk_t = jnp.tile(g_k[l], q_w // d_qk).reshape(1, q_w)

        mod = _ada(jnp.concatenate([c_prompt, c_sample], axis=0), w_ada[l], b_ada[l])
        mod_p = mod[:B].reshape(B, 6, D)
        mod_s = mod[B:].reshape(Bd, 6, D)

        q, k, kb, v, vb, opool, pst = _pre(
            yp, mod_p, g_attn[l], w_in_b, gq_t, gk_t, gmat, cos_p, sa_p, sb_p, w_pool_b,
            pool_scale[l], None, nseq=1, T=tp, n_heads=n_heads, pos_base=0)
        oatt = _attp(q, kb, vb, lamvec, g_sub[l], n_seq=B, seq=S, n_heads=n_heads, lam_init=lam_init)
        yp, cst = _post(yp, oatt, opool, mod_p, g_ffn[l], w_out_b, w_up_b, conv_w[l], conv_b[l],
                        w_dn_b, None, nseq=1, T=tp)
        outs[0].append(k.reshape(B, S, n_heads, LANES))
        outs[1].append(v.reshape(B, S, n_heads, LANES))
        outs[2].append(pst[:, 1:, :])
        outs[3].append(cst[:, CONV_PREFIX_ROWS - (CONV_W - 1):, :])

        ppre = jnp.pad(state_pool[l], ((0, 0), (POOL_PREFIX_ROWS - state_pool.shape[2], 0), (0, 0)))
        cpre = jnp.pad(state_conv[l], ((0, 0), (CONV_PREFIX_ROWS - state_conv.shape[2], 0), (0, 0)))
        q, k, kb, v, vb, opool, pst = _pre(
            ys, mod_s, g_attn[l], w_in_b, gq_t, gk_t, gmat, cos_s, sa_s, sb_s, w_pool_b,
            pool_scale[l], ppre, nseq=ns, T=T, n_heads=n_heads, pos_base=past_len)
        kn = jnp.pad(kb.reshape(Bd, T * n_heads, LANES), ((0, 0), (0, n_new - T * n_heads), (0, 0)))
        vn = jnp.pad(vb.reshape(Bd, T * n_heads, LANES), ((0, 0), (0, n_new - T * n_heads), (0, 0)))
        oatt = _atts(q.astype(F32).reshape(Bd, T, q_w), kn, vn, ck, cv, page_table, l * n_pool,
                     lamvec, g_sub[l], n_heads=n_heads, lam_init=lam_init)
        ys, cst = _post(ys, oatt.reshape(Bd * T, q_w), opool, mod_s, g_ffn[l], w_out_b, w_up_b,
                        conv_w[l], conv_b[l], w_dn_b, cpre, nseq=ns_post, T=T)
        outs[4].append(k.reshape(Bd, T, n_heads, LANES))
        outs[5].append(v.reshape(Bd, T, n_heads, LANES))
        outs[6].append(pst[:, 1:, :])
        outs[7].append(cst[:, CONV_PREFIX_ROWS - (CONV_W - 1):, :])

    return (yp, ys) + tuple(o[0][None] if depth == 1 else jnp.stack(o) for o in outs)
```

```python
import functools
import math

import jax
import jax.numpy as jnp
from jax import lax
from jax.experimental import pallas as pl
from jax.experimental.pallas import tpu as pltpu

F32 = jnp.float32
BF16 = jnp.bfloat16

POOL_WINDOWS = (2, 4, 8, 16)
POOL_PREFIX_ROWS = 16
CONV_W = 3
CONV_PREFIX_ROWS = 8
ROPE_THETA = 500000.0
EPS = 1e-6
LANES = 128
NEG_BIG = -1e30
VMEM_LIMIT = 56 * 1024 * 1024


def _silu(x):
    return x / (1.0 + jnp.exp(-x))


def _rms(x, axis=-1):
    return x * lax.rsqrt(jnp.mean(x * x, axis=axis, keepdims=True) + EPS)


def _dot(a, b):
    return jnp.dot(a, b, preferred_element_type=F32)


def _dot_nt(a, b):
    return lax.dot_general(a, b, (((1,), (1,)), ((), ())), preferred_element_type=F32)


def _split_dot(x, w_bf16):
    hi = x.astype(BF16)
    lo = (x - hi.astype(F32)).astype(BF16)
    return _dot(hi, w_bf16) + _dot(lo, w_bf16)


def _ada_kernel(c_ref, w_ref, b_ref, o_ref):
    s = _silu(c_ref[...]).astype(BF16)
    o_ref[...] = _dot(s, w_ref[...].astype(BF16)) + b_ref[...]


def _ada(c, w_ada, b_ada):
    nb, d = c.shape
    n = w_ada.shape[1]
    bn = 1024 if n % 1024 == 0 else n
    return pl.pallas_call(
        _ada_kernel,
        grid=(n // bn,),
        in_specs=[pl.BlockSpec((nb, d), lambda j: (0, 0)),
                  pl.BlockSpec((d, bn), lambda j: (0, j)),
                  pl.BlockSpec((1, bn), lambda j: (0, j))],
        out_specs=pl.BlockSpec((nb, bn), lambda j: (0, j)),
        out_shape=jax.ShapeDtypeStruct((nb, n), F32),
        compiler_params=pltpu.CompilerParams(dimension_semantics=("arbitrary",),
                                             vmem_limit_bytes=VMEM_LIMIT),
        name="ada",
    )(c, w_ada, b_ada.reshape(1, n))


def _pre_kernel(*refs, nseq, T, n_heads, att_w, pool_w, pos_base, has_prefix):
    if has_prefix:
        (x_ref, mod_ref, gattn_ref, win_ref, gq_ref, gk_ref, gmat_ref, cos_ref, sa_ref, sb_ref,
         wpool_ref, pscale_ref, prefix_ref,
         q_ref, k_ref, kb_ref, v_ref, vb_ref, op_ref, pst_ref) = refs
    else:
        (x_ref, mod_ref, gattn_ref, win_ref, gq_ref, gk_ref, gmat_ref, cos_ref, sa_ref, sb_ref,
         wpool_ref, pscale_ref,
         q_ref, k_ref, kb_ref, v_ref, vb_ref, op_ref, pst_ref) = refs
    j = pl.program_id(1)
    M = nseq * T
    q_w = n_heads * LANES

    x = x_ref[...]
    d = x.shape[-1]
    h = _rms(x) * gattn_ref[...] * (1.0 + mod_ref[:, 1:2, :]) + mod_ref[:, 0:1, :]
    hb = h.reshape(M, d).astype(BF16)
    o_k, o_v, o_u = q_w, 2 * q_w, 2 * q_w + att_w
    pq = _dot(hb, win_ref[:, 0:o_k])
    pk = _dot(hb, win_ref[:, o_k:o_v])
    u = _dot(hb, win_ref[:, o_u:o_u + pool_w])
    v = _dot(hb, win_ref[:, o_v:o_u])

    cos = jnp.broadcast_to(cos_ref[...][None], (nseq, T, LANES)).reshape(M, LANES)
    sa = jnp.broadcast_to(sa_ref[...][None], (nseq, T, LANES)).reshape(M, LANES)
    sb = jnp.broadcast_to(sb_ref[...][None], (nseq, T, LANES)).reshape(M, LANES)

    def qk_norm_rope(t, g):
        ms = _split_dot(t * t, gmat_ref[...])
        t = t * lax.rsqrt(ms + EPS) * g
        outs = []
        for hh in range(n_heads):
            th = t[:, hh * LANES:(hh + 1) * LANES]
            half = 8
            outs.append(th * cos + pltpu.roll(th, LANES - half, axis=1) * sa
                        + pltpu.roll(th, half, axis=1) * sb)
        return jnp.concatenate(outs, axis=1)

    q = qk_norm_rope(pq, gq_ref[...])
    k = qk_norm_rope(pk, gk_ref[...])

    scale = (LANES // 2) ** -0.5
    q_ref[...] = (q * scale).astype(BF16)
    kb_ref[...] = k.astype(BF16)
    vb_ref[...] = v.astype(BF16)
    for hh in range(n_heads):
        k_ref[pl.ds(hh, M, stride=n_heads), :] = k[:, hh * LANES:(hh + 1) * LANES]
        v_ref[pl.ds(hh, M, stride=n_heads), :] = v[:, hh * LANES:(hh + 1) * LANES]

    P = POOL_PREFIX_ROWS
    if has_prefix:
        pre = prefix_ref[...]
    else:
        @pl.when(j == 0)
        def _():
            pst_ref[...] = jnp.zeros_like(pst_ref)
        pre = pst_ref[...]
    ext3 = jnp.concatenate([pre, u.reshape(nseq, T, pool_w)], axis=1)
    pst_ref[...] = ext3[:, T:T + P, :]
    ext = ext3.reshape(nseq * (P + T), pool_w)

    pos = pos_base + j * T + lax.broadcasted_iota(jnp.int32, (nseq, T, LANES), 1)
    g_w = pool_w // len(POOL_WINDOWS)
    outs = []
    for g, w in enumerate(POOL_WINDOWS):
        e = ext[:, g * g_w:(g + 1) * g_w]
        s = e
        sh = 1
        while sh < w:
            s = s + pltpu.roll(s, sh, axis=0)
            sh *= 2
        s3 = s.reshape(nseq, P + T, g_w)[:, P:, :]
        e3 = e.reshape(nseq, P + T, g_w)[:, P:, :]
        cnt = jnp.minimum(w, pos + 1).astype(F32)
        pooled = (s3 / cnt - e3).reshape(M, g_w).astype(BF16)
        outs.append(_dot(pooled, wpool_ref[g]))
    op_ref[...] = (jnp.concatenate(outs, axis=1) * pscale_ref[...]).astype(BF16)


def _pre(x, mod, g_attn, w_in, gq_t, gk_t, gmat, cos, sa, sb, w_pool, pool_scale, prefix,
         *, nseq, T, n_heads, pos_base):
    ns_total, t_total, d = x.shape
    in_w = w_in.shape[1]
    pool_w = pool_scale.shape[-1]
    q_w = n_heads * LANES
    att_w = in_w - 2 * q_w - pool_w
    assert pool_w // len(POOL_WINDOWS) == LANES and att_w == q_w
    ni, nj = ns_total // nseq, t_total // T
    M = nseq * T
    n_tok = ns_total * t_total
    has_prefix = prefix is not None
    assert has_prefix or nseq == 1

    const = lambda shape: pl.BlockSpec(shape, lambda i, j: (0,) * len(shape),
                                       pipeline_mode=pl.Buffered(1))
    tok = lambda w: pl.BlockSpec((M, w), lambda i, j: (i * nj + j, 0))
    in_specs = [
        pl.BlockSpec((nseq, T, d), lambda i, j: (i, j, 0)),
        pl.BlockSpec((nseq, 6, d), lambda i, j: (i, 0, 0)),
        const((1, d)), const((d, in_w)), const((1, q_w)), const((1, q_w)), const((q_w, q_w)),
        pl.BlockSpec((T, LANES), lambda i, j: (j, 0)),
        pl.BlockSpec((T, LANES), lambda i, j: (j, 0)),
        pl.BlockSpec((T, LANES), lambda i, j: (j, 0)),
        const(w_pool.shape), const((1, pool_w)),
    ]
    args = [x, mod, g_attn.reshape(1, d), w_in, gq_t, gk_t, gmat, cos, sa, sb,
            w_pool, pool_scale.reshape(1, pool_w)]
    if has_prefix:
        in_specs.append(pl.BlockSpec((nseq, POOL_PREFIX_ROWS, pool_w), lambda i, j: (i, 0, 0)))
        args.append(prefix)
    out_shape = (
        jax.ShapeDtypeStruct((n_tok, q_w), BF16),
        jax.ShapeDtypeStruct((n_tok * n_heads, LANES), F32),
        jax.ShapeDtypeStruct((n_tok, q_w), BF16),
        jax.ShapeDtypeStruct((n_tok * n_heads, LANES), F32),
        jax.ShapeDtypeStruct((n_tok, att_w), BF16),
        jax.ShapeDtypeStruct((n_tok, pool_w), BF16),
        jax.ShapeDtypeStruct((ns_total, POOL_PREFIX_ROWS, pool_w), F32),
    )
    tok_head = pl.BlockSpec((M * n_heads, LANES), lambda i, j: (i * nj + j, 0))
    out_specs = (tok(q_w), tok_head, tok(q_w), tok_head, tok(att_w), tok(pool_w),
                 pl.BlockSpec((nseq, POOL_PREFIX_ROWS, pool_w), lambda i, j: (i, 0, 0)))
    kern = functools.partial(_pre_kernel, nseq=nseq, T=T, n_heads=n_heads, att_w=att_w,
                             pool_w=pool_w, pos_base=pos_base, has_prefix=has_prefix)
    return pl.pallas_call(
        kern, grid=(ni, nj), in_specs=in_specs, out_specs=out_specs, out_shape=out_shape,
        compiler_params=pltpu.CompilerParams(dimension_semantics=("arbitrary", "arbitrary"),
                                             vmem_limit_bytes=VMEM_LIMIT),
        name="pre",
    )(*args)


def _lam_from(lamvec_ref, lam_init):
    a = jnp.sum(lamvec_ref[0:1, :] * lamvec_ref[1:2, :], axis=-1, keepdims=True)
    b = jnp.sum(lamvec_ref[2:3, :] * lamvec_ref[3:4, :], axis=-1, keepdims=True)
    return jnp.exp(a) - jnp.exp(b) + lam_init


def _attp_kernel(lamvec_ref, q_ref, k_ref, v_ref, gsub_ref, o_ref, qs_ref, m_ref, l_ref, acc_ref,
                 *, tq, nq, lam_init):
    half = LANES // 2
    lane = lax.broadcasted_iota(jnp.int32, (tq, LANES), 1)
    for qi in range(nq):
        q = q_ref[qi * tq:(qi + 1) * tq, :]
        zero = jnp.zeros_like(q)
        qs_ref[qi, 0:tq, :] = jnp.where(lane < half, q, zero)
        qs_ref[qi, tq:2 * tq, :] = jnp.where(lane >= half, q, zero)
    m_ref[...] = jnp.full_like(m_ref, NEG_BIG)
    l_ref[...] = jnp.zeros_like(l_ref)
    acc_ref[...] = jnp.zeros_like(acc_ref)

    row = lax.broadcasted_iota(jnp.int32, (2 * tq, tq), 0)
    col = lax.broadcasted_iota(jnp.int32, (2 * tq, tq), 1)
    causal = col <= jnp.where(row >= tq, row - tq, row)

    def scores(qi, ki):
        s = _dot_nt(qs_ref[qi], k_ref[ki * tq:(ki + 1) * tq, :])
        return jnp.where(causal, s, NEG_BIG) if qi == ki else s

    def update(qi, ki, s):
        m_prev = m_ref[qi]
        m_new = jnp.maximum(m_prev, jnp.max(s, axis=1, keepdims=True))
        alpha = jnp.exp(m_prev - m_new)
        p = jnp.exp(s - jnp.concatenate([m_new] * (tq // LANES), axis=1))
        l_ref[qi] = alpha * l_ref[qi] + jnp.sum(p, axis=1, keepdims=True)
        acc_ref[qi] = alpha * acc_ref[qi] + _dot(p.astype(BF16), v_ref[ki * tq:(ki + 1) * tq, :])
        m_ref[qi] = m_new

    pairs = [(qi, ki) for ki in range(nq) for qi in range(ki, nq)]
    s = scores(*pairs[0])
    for i, (qi, ki) in enumerate(pairs):
        s_next = scores(*pairs[i + 1]) if i + 1 < len(pairs) else None
        update(qi, ki, s)
        s = s_next

    lam = _lam_from(lamvec_ref, lam_init)
    for qi in range(nq):
        o = acc_ref[qi] / l_ref[qi]
        dlt = o[0:tq, :] - lam * o[tq:2 * tq, :]
        o_ref[qi * tq:(qi + 1) * tq, :] = (_rms(dlt) * gsub_ref[...]
                                           * (1.0 - lam_init)).astype(o_ref.dtype)


def _attp(q, kb, vb, lamvec, g_sub, *, n_seq, seq, n_heads, lam_init):
    tq = 512 if seq % 512 == 0 else seq
    nq = seq // tq
    n_tok = q.shape[0]
    seq_head = pl.BlockSpec((seq, LANES), lambda b, h: (b, h))
    return pl.pallas_call(
        functools.partial(_attp_kernel, tq=tq, nq=nq, lam_init=lam_init),
        grid=(n_seq, n_heads),
        in_specs=[pl.BlockSpec((4, LANES), lambda b, h: (0, 0)),
                  seq_head, seq_head, seq_head,
                  pl.BlockSpec((1, LANES), lambda b, h: (0, 0))],
        out_specs=seq_head,
        out_shape=jax.ShapeDtypeStruct((n_tok, n_heads * LANES), BF16),
        scratch_shapes=[pltpu.VMEM((nq, 2 * tq, LANES), BF16),
                        pltpu.VMEM((nq, 2 * tq, LANES), F32),
                        pltpu.VMEM((nq, 2 * tq, LANES), F32),
                        pltpu.VMEM((nq, 2 * tq, LANES), F32)],
        compiler_params=pltpu.CompilerParams(dimension_semantics=("arbitrary", "arbitrary"),
                                             vmem_limit_bytes=VMEM_LIMIT),
        name="attp",
    )(lamvec, q, kb, vb, g_sub.reshape(1, LANES))


PAGES_PER_CHUNK = 8
KV_RING_SLOTS = 4


def _lane_tile(x, n):
    return x if n == 1 else jnp.concatenate([x] * n, axis=1)


def _stack_q(q, n_heads, T):
    half = LANES // 2
    lane = lax.broadcasted_iota(jnp.int32, (T, LANES), 1)
    rows = []
    for hh in range(n_heads):
        qh = q[:, hh * LANES:(hh + 1) * LANES]
        zero = jnp.zeros_like(qh)
        rows += [jnp.where(lane < half, qh, zero), jnp.where(lane >= half, qh, zero)]
    return jnp.concatenate(rows, axis=0)


def _softmax_partial(s, v):
    mj = jnp.broadcast_to(jnp.max(s, axis=1, keepdims=True), (s.shape[0], LANES))
    p = jnp.exp(s - _lane_tile(mj, s.shape[1] // LANES))
    return mj, jnp.sum(p, axis=1, keepdims=True), _dot(p.astype(BF16), v)


def _merge_partials(state, parts):
    m_prev, l_prev, acc_prev = state
    m_new = m_prev
    for mj, _, _ in parts:
        m_new = jnp.maximum(m_new, mj)
    alpha = jnp.exp(m_prev - m_new)
    l_new = alpha * l_prev
    acc = alpha * acc_prev
    for mj, lj, oj in parts:
        wj = jnp.exp(mj - m_new)
        l_new = l_new + wj * lj
        acc = acc + wj * oj
    return m_new, l_new, acc


def _page_copies(pt_ref, ck_hbm, cv_hbm, kbuf, vbuf, sems, b, chunk, slot, *, layer_base):
    copies = []
    for jj in range(PAGES_PER_CHUNK):
        page = layer_base + pt_ref[b, chunk * PAGES_PER_CHUNK + jj]
        copies.append(pltpu.make_async_copy(ck_hbm.at[page], kbuf.at[slot, jj], sems.at[0, slot]))
        copies.append(pltpu.make_async_copy(cv_hbm.at[page], vbuf.at[slot, jj], sems.at[1, slot]))
    return copies


def _chunk_scores(qall, kbuf, slot):
    return [_dot_nt(qall, kbuf[slot, jj].astype(BF16)) for jj in range(PAGES_PER_CHUNK)]


def _chunk_accumulate(scores, vbuf, slot, head_ok, state):
    parts = [_softmax_partial(jnp.where(head_ok, s, NEG_BIG), vbuf[slot, jj].astype(BF16))
             for jj, s in enumerate(scores)]
    return _merge_partials(state, parts)


def _attend_new_and_finish(qall, state, kn_ref, vn_ref, lamvec_ref, gsub_ref, o_ref,
                           *, n_heads, T, lam_init):
    R, n_new = qall.shape[0], kn_ref.shape[0]
    nrow = lax.broadcasted_iota(jnp.int32, (R, n_new), 0)
    ncol = lax.broadcasted_iota(jnp.int32, (R, n_new), 1)
    ok = ((ncol % n_heads) == (nrow // (2 * T))) & ((ncol // n_heads) <= (nrow % T)) \
        & (ncol < T * n_heads)
    s = jnp.where(ok, _dot_nt(qall, kn_ref[...]), NEG_BIG)
    _, l, acc = _merge_partials(state, [_softmax_partial(s, vn_ref[...])])
    lam = _lam_from(lamvec_ref, lam_init)
    o = acc / l
    for hh in range(n_heads):
        dlt = o[(2 * hh) * T:(2 * hh + 1) * T, :] - lam * o[(2 * hh + 1) * T:(2 * hh + 2) * T, :]
        o_ref[:, hh * LANES:(hh + 1) * LANES] = _rms(dlt) * gsub_ref[...] * (1.0 - lam_init)


def _paged_attention_step(b, n_seq, pt_ref, lamvec_ref, q_ref, kn_ref, vn_ref, gsub_ref, ck_hbm, cv_hbm,
                          o_ref, kbuf, vbuf, sems, *, n_heads, T, n_chunks, layer_base, lam_init,
                          while_in_flight=None, state_ref=None):
    assert n_chunks % KV_RING_SLOTS == 0
    ahead = KV_RING_SLOTS - 1
    copies = functools.partial(_page_copies, pt_ref, ck_hbm, cv_hbm, kbuf, vbuf, sems,
                               layer_base=layer_base)

    @pl.when(b == 0)
    def _():
        for c in range(ahead):
            for cp in copies(0, c, c):
                cp.start()

    R = n_heads * 2 * T
    rows_page = kbuf.shape[2]
    row = lax.broadcasted_iota(jnp.int32, (R, rows_page), 0)
    col = lax.broadcasted_iota(jnp.int32, (R, rows_page), 1)
    head_ok = (col % n_heads) == (row // (2 * T))
    qall = _stack_q(q_ref[...], n_heads, T).astype(BF16)
    state = (jnp.full((R, LANES), NEG_BIG, F32), jnp.zeros((R, LANES), F32), jnp.zeros((R, LANES), F32))
    if state_ref is not None:
        for i in range(3):
            state_ref[i] = state[i]

    def attend(c, state):
        for cp in copies(b, c, c % KV_RING_SLOTS):
            cp.wait()
        scores = _chunk_scores(qall, kbuf, c % KV_RING_SLOTS)
        return _chunk_accumulate(scores, vbuf, c % KV_RING_SLOTS, head_ok, state)

    for c in range(n_chunks):
        nxt = c + ahead
        if nxt < n_chunks:
            for cp in copies(b, nxt, nxt % KV_RING_SLOTS):
                cp.start()
        else:
            @pl.when(b + 1 < n_seq)
            def _():
                for cp in copies(b + 1, nxt - n_chunks, nxt % KV_RING_SLOTS):
                    cp.start()
        if while_in_flight is not None:
            while_in_flight(c)
        if state_ref is None:
            state = attend(c, state)
        else:
            @pl.when(b < n_seq)
            def _():
                new = attend(c, tuple(state_ref[i] for i in range(3)))
                for i in range(3):
                    state_ref[i] = new[i]
    if state_ref is not None:
        state = tuple(state_ref[i] for i in range(3))
    _attend_new_and_finish(qall, state, kn_ref, vn_ref, lamvec_ref, gsub_ref, o_ref,
                           n_heads=n_heads, T=T, lam_init=lam_init)


def _atts_kernel(pt_ref, lamvec_ref, q_ref, kn_ref, vn_ref, gsub_ref, ck_hbm, cv_hbm, o_ref,
                 kbuf, vbuf, sems, **kw):
    _paged_attention_step(pl.program_id(0), pl.num_programs(0), pt_ref, lamvec_ref, q_ref, kn_ref,
                          vn_ref, gsub_ref, ck_hbm, cv_hbm, o_ref, kbuf, vbuf, sems, **kw)


def _kv_ring_scratch(rows_page):
    return [pltpu.VMEM((KV_RING_SLOTS, PAGES_PER_CHUNK, rows_page, LANES), F32),
            pltpu.VMEM((KV_RING_SLOTS, PAGES_PER_CHUNK, rows_page, LANES), F32),
            pltpu.SemaphoreType.DMA((2, KV_RING_SLOTS))]


def _atts(q3, kn, vn, cache_k, cache_v, page_table, layer_base, lamvec, g_sub,
          *, n_heads, lam_init):
    bd, T, qw = q3.shape
    n_pages = page_table.shape[1]
    rows_page = cache_k.shape[1]
    n_new = kn.shape[1]
    assert n_pages % (PAGES_PER_CHUNK * KV_RING_SLOTS) == 0
    in_specs = [
        pl.BlockSpec((4, LANES), lambda b, pt: (0, 0)),
        pl.BlockSpec((None, T, qw), lambda b, pt: (b, 0, 0)),
        pl.BlockSpec((None, n_new, LANES), lambda b, pt: (b, 0, 0)),
        pl.BlockSpec((None, n_new, LANES), lambda b, pt: (b, 0, 0)),
        pl.BlockSpec((1, LANES), lambda b, pt: (0, 0)),
        pl.BlockSpec(memory_space=pl.ANY),
        pl.BlockSpec(memory_space=pl.ANY),
    ]
    grid_spec = pltpu.PrefetchScalarGridSpec(
        num_scalar_prefetch=1, grid=(bd,), in_specs=in_specs,
        out_specs=pl.BlockSpec((None, T, qw), lambda b, pt: (b, 0, 0)),
        scratch_shapes=_kv_ring_scratch(rows_page),
    )
    kern = functools.partial(_atts_kernel, n_heads=n_heads, T=T,
                             n_chunks=n_pages // PAGES_PER_CHUNK, layer_base=layer_base,
                             lam_init=lam_init)
    return pl.pallas_call(
        kern, grid_spec=grid_spec,
        out_shape=jax.ShapeDtypeStruct((bd, T, qw), F32),
        compiler_params=pltpu.CompilerParams(dimension_semantics=("arbitrary",),
                                             vmem_limit_bytes=VMEM_LIMIT),
        name="atts",
    )(page_table, lamvec, q3, kn, vn, g_sub.reshape(1, LANES), cache_k, cache_v)


def _post_kernel(*refs, nseq, T, fc, has_prefix, paged):
    refs = list(refs)
    pt_ref = refs.pop(0) if paged else None
    x_ref, oa_ref, op_ref, mod_ref, gffn_ref, wout_ref, wup_ref, cw_ref, cb_ref, wdn_ref = refs[:10]
    del refs[:10]
    cpre_ref = refs.pop(0) if has_prefix else None
    if paged:
        lamvec_ref, q_ref, kn_ref, vn_ref, gsub_ref, ck_hbm, cv_hbm = refs[:7]
        del refs[:7]
    y_ref, cst_ref = refs[:2]
    del refs[:2]
    oatt_ref = refs.pop(0) if paged else None
    h2_ref, a_ref, hm_ref = refs[:3]
    del refs[:3]
    if paged:
        kbuf, vbuf, sems, state_ref = refs
    if not has_prefix:
        cpre_ref = cst_ref
    j = pl.program_id(1)
    M = nseq * T
    d = x_ref.shape[-1]
    ff = wdn_ref.shape[0]
    att_w = oa_ref.shape[1]
    C = CONV_PREFIX_ROWS

    mix = _dot(oa_ref[...].astype(BF16), wout_ref[0:att_w, :]) \
        + _dot(op_ref[...].astype(BF16), wout_ref[att_w:, :])
    x1 = x_ref[...] + mod_ref[:, 2:3, :] * mix.reshape(nseq, T, d)
    y_ref[...] = x1
    h2 = _rms(x1) * gffn_ref[...] * (1.0 + mod_ref[:, 4:5, :]) + mod_ref[:, 3:4, :]
    h2_ref[...] = h2.reshape(M, d).astype(BF16)

    if not has_prefix:
        @pl.when(j == 0)
        def _():
            cst_ref[...] = jnp.zeros_like(cst_ref)

    def conv(a, col):
        pre = cpre_ref[:, :, col:col + fc]
        ext3 = jnp.concatenate([pre, a.reshape(nseq, T, fc)], axis=1)
        cst_ref[:, :, col:col + fc] = ext3[:, T:T + C, :]
        ext = ext3.reshape(nseq * (C + T), fc)
        y = (cb_ref[:, col:col + fc]
             + pltpu.roll(ext, 2, axis=0) * cw_ref[0:1, col:col + fc]
             + pltpu.roll(ext, 1, axis=0) * cw_ref[1:2, col:col + fc]
             + ext * cw_ref[2:3, col:col + fc])
        return y.reshape(nseq, C + T, fc)[:, C:, :].reshape(M, fc)

    nc = ff // fc

    def up_proj(ci):
        hb = h2_ref[...]
        a_ref[ci % 2, :, 0:fc] = _dot(hb, wup_ref[:, ci * fc:(ci + 1) * fc])
        a_ref[ci % 2, :, fc:2 * fc] = _dot(hb, wup_ref[:, ff + ci * fc:ff + (ci + 1) * fc])

    def conv_gate(ci):
        g = conv(a_ref[ci % 2, :, 0:fc], ci * fc)
        up = conv(a_ref[ci % 2, :, fc:2 * fc], ff + ci * fc)
        hm_ref[:, ci * fc:(ci + 1) * fc] = (_silu(g) * up).astype(BF16)

    def ffn_chunks(lo, hi):
        for ci in range(lo, hi):
            if ci + 1 < nc:
                up_proj(ci + 1)
            conv_gate(ci)

    up_proj(0)
    if paged:
        half_ffn, half_att = (nc + 1) // 2, paged["n_chunks"] // 2
        runs = {0: (0, half_ffn), half_att: (half_ffn, nc)}
        step = pl.program_id(0) * pl.num_programs(1) + j
        n_steps = pl.num_programs(0) * pl.num_programs(1)
        _paged_attention_step(step, n_steps, pt_ref, lamvec_ref, q_ref, kn_ref, vn_ref, gsub_ref,
                              ck_hbm, cv_hbm, oatt_ref, kbuf, vbuf, sems,
                              while_in_flight=lambda c: ffn_chunks(*runs[c]) if c in runs else None,
                              state_ref=state_ref, **paged)
    else:
        ffn_chunks(0, nc)
    f = _dot(hm_ref[...], wdn_ref[...])
    y_ref[...] = y_ref[...] + mod_ref[:, 5:6, :] * f.reshape(nseq, T, d)


def _post(x, oa, op, mod, g_ffn, w_out, w_up, conv_w, conv_b, w_down, cprefix, *, nseq, T,
          paged=None):
    ns_total, t_total, d = x.shape
    ff = w_down.shape[0]
    fc = 256 if ff % 256 == 0 else ff
    ni, nj = ns_total // nseq, t_total // T
    M = nseq * T
    has_prefix = cprefix is not None
    assert has_prefix or nseq == 1
    C = CONV_PREFIX_ROWS

    const = lambda shape: pl.BlockSpec(shape, lambda i, j, *_: (0,) * len(shape),
                                       pipeline_mode=pl.Buffered(1))
    tok = lambda w: pl.BlockSpec((M, w), lambda i, j, *_: (i * nj + j, 0))
    in_specs = [
        pl.BlockSpec((nseq, T, d), lambda i, j, *_: (i, j, 0)),
        tok(oa.shape[1]), tok(op.shape[1]),
        pl.BlockSpec((nseq, 6, d), lambda i, j, *_: (i, 0, 0)),
        const((1, d)), const(w_out.shape), const(w_up.shape), const((CONV_W, 2 * ff)),
        const((1, 2 * ff)), const(w_down.shape),
    ]
    args = [x, oa, op, mod, g_ffn.reshape(1, d), w_out, w_up, conv_w, conv_b.reshape(1, 2 * ff), w_down]
    if has_prefix:
        in_specs.append(pl.BlockSpec((nseq, C, 2 * ff), lambda i, j, *_: (i, 0, 0)))
        args.append(cprefix)
    out_specs = [pl.BlockSpec((nseq, T, d), lambda i, j, *_: (i, j, 0)),
                 pl.BlockSpec((nseq, C, 2 * ff), lambda i, j, *_: (i, 0, 0))]
    out_shape = [jax.ShapeDtypeStruct(x.shape, F32),
                 jax.ShapeDtypeStruct((ns_total, C, 2 * ff), F32)]
    scratch = [pltpu.VMEM((M, d), BF16),
               pltpu.VMEM((2, M, 2 * fc), F32),
               pltpu.VMEM((M, ff), BF16)]
    prefetch = []
    cfg = None
    if paged is not None:
        page_table, lamvec, q3, kn, vn, g_sub, cache_k, cache_v, cfg = paged
        bd, t_s, qw = q3.shape
        assert bd == ni * nj
        seq_blk = lambda r, w: pl.BlockSpec((None, r, w), lambda i, j, *_: (i * nj + j, 0, 0))
        in_specs += [pl.BlockSpec((4, LANES), lambda i, j, *_: (0, 0)),
                     seq_blk(t_s, qw), seq_blk(kn.shape[1], LANES), seq_blk(vn.shape[1], LANES),
                     pl.BlockSpec((1, LANES), lambda i, j, *_: (0, 0)),
                     pl.BlockSpec(memory_space=pl.ANY), pl.BlockSpec(memory_space=pl.ANY)]
        args += [lamvec, q3, kn, vn, g_sub.reshape(1, LANES), cache_k, cache_v]
        out_specs.append(seq_blk(t_s, qw))
        out_shape.append(jax.ShapeDtypeStruct((bd, t_s, qw), F32))
        scratch += _kv_ring_scratch(cache_k.shape[1])
        scratch.append(pltpu.VMEM((3, cfg["n_heads"] * 2 * t_s, LANES), F32))
        prefetch = [page_table]
    grid_spec = pltpu.PrefetchScalarGridSpec(
        num_scalar_prefetch=len(prefetch), grid=(ni, nj), in_specs=in_specs, out_specs=out_specs,
        scratch_shapes=scratch)
    return pl.pallas_call(
        functools.partial(_post_kernel, nseq=nseq, T=T, fc=fc, has_prefix=has_prefix, paged=cfg),
        grid_spec=grid_spec, out_shape=out_shape,
        compiler_params=pltpu.CompilerParams(dimension_semantics=("arbitrary", "arbitrary"),
                                             vmem_limit_bytes=VMEM_LIMIT),
        name="post",
    )(*prefetch, *args)


def _rope_tables(pos, d_qk, rot_dim):
    half = rot_dim // 2
    inv = ROPE_THETA ** (-jnp.arange(0, rot_dim, 2, dtype=F32) / rot_dim)
    ang = pos.astype(F32)[:, None] * inv[None, :]
    cos, sin = jnp.cos(ang), jnp.sin(ang)
    l64 = jnp.arange(LANES) % d_qk
    idx = l64 % half
    cos_t = jnp.where(l64 < rot_dim, cos[:, idx], 1.0)
    sa = jnp.where(l64 < half, -sin[:, idx], 0.0)
    sb = jnp.where((l64 >= half) & (l64 < rot_dim), sin[:, idx], 0.0)
    return cos_t, sa, sb


def _tile_rows(n, target):
    t = min(n, target)
    while n % t:
        t //= 2
    return t


def kernel(x_prompt, x_sample, cache_k, cache_v, state_pool, state_conv, page_table, c_prompt, c_sample, w_ada, b_ada, g_attn, w_in, g_q, g_k, lam_q1, lam_k1, lam_q2, lam_k2, g_sub, w_pool, pool_scale, w_out, g_ffn, w_up, conv_w, conv_b, w_down):
    B, S, D = x_prompt.shape
    Bd, T, _ = x_sample.shape
    depth, n_pool, page_size, n_heads, dk2 = cache_k.shape
    d_qk = g_q.shape[-1]
    assert dk2 == 2 * d_qk == LANES and cache_v.shape[-1] == LANES
    rot_dim = d_qk // 4
    assert rot_dim == 16
    past_len = page_table.shape[1] * page_size
    pool_w = pool_scale.shape[-1]
    ff = w_down.shape[1]
    q_w = n_heads * LANES

    ck = cache_k.reshape(depth * n_pool, page_size * n_heads, LANES)
    cv = cache_v.reshape(depth * n_pool, page_size * n_heads, LANES)

    cos_p, sa_p, sb_p = _rope_tables(jnp.arange(S), d_qk, rot_dim)
    cos_s, sa_s, sb_s = _rope_tables(past_len + jnp.arange(T), d_qk, rot_dim)
    blk = jnp.arange(q_w) // d_qk
    gmat = jnp.where(blk[:, None] == blk[None, :], 1.0 / d_qk, 0.0).astype(BF16)

    tp = _tile_rows(S, 512)
    ns = _tile_rows(Bd, max(1, 512 // T))
    ns_post = _tile_rows(Bd, 16)
    n_new = max(LANES, T * n_heads)

    yp, ys = x_prompt, x_sample
    outs = [[] for _ in range(8)]
    for l in range(depth):
        lam_init = 0.8 - 0.6 * math.exp(-0.3 * l)
        lamvec = jnp.pad(jnp.stack([lam_q1[l], lam_k1[l], lam_q2[l], lam_k2[l]]).astype(F32),
                         ((0, 0), (0, LANES - d_qk)))
        w_in_b, w_out_b = w_in[l].astype(BF16), w_out[l].astype(BF16)
        w_up_b, w_dn_b, w_pool_b = w_up[l].astype(BF16), w_down[l].astype(BF16), w_pool[l].astype(BF16)
        gq_t = jnp.tile(g_q[l], q_w // d_qk).reshape(1, q_w)
        gk_t = jnp.tile(g_k[l], q_w // d_qk).reshape(1, q_w)

        mod = _ada(jnp.concatenate([c_prompt, c_sample], axis=0), w_ada[l], b_ada[l])
        mod_p = mod[:B].reshape(B, 6, D)
        mod_s = mod[B:].reshape(Bd, 6, D)

        q, k, kb, v, vb, opool, pst = _pre(
            yp, mod_p, g_attn[l], w_in_b, gq_t, gk_t, gmat, cos_p, sa_p, sb_p, w_pool_b,
            pool_scale[l], None, nseq=1, T=tp, n_heads=n_heads, pos_base=0)
        oatt = _attp(q, kb, vb, lamvec, g_sub[l], n_seq=B, seq=S, n_heads=n_heads, lam_init=lam_init)
        outs[0].append(k.reshape(B, S, n_heads, LANES))
        outs[1].append(v.reshape(B, S, n_heads, LANES))
        outs[2].append(pst[:, 1:, :])

        ppre = jnp.pad(state_pool[l], ((0, 0), (POOL_PREFIX_ROWS - state_pool.shape[2], 0), (0, 0)))
        cpre = jnp.pad(state_conv[l], ((0, 0), (CONV_PREFIX_ROWS - state_conv.shape[2], 0), (0, 0)))
        q, k, kb, v, vb, opool_s, pst = _pre(
            ys, mod_s, g_attn[l], w_in_b, gq_t, gk_t, gmat, cos_s, sa_s, sb_s, w_pool_b,
            pool_scale[l], ppre, nseq=ns, T=T, n_heads=n_heads, pos_base=past_len)
        kn = jnp.pad(kb.reshape(Bd, T * n_heads, LANES), ((0, 0), (0, n_new - T * n_heads), (0, 0)))
        vn = jnp.pad(vb.reshape(Bd, T * n_heads, LANES), ((0, 0), (0, n_new - T * n_heads), (0, 0)))
        q3 = q.astype(F32).reshape(Bd, T, q_w)
        outs[4].append(k.reshape(Bd, T, n_heads, LANES))
        outs[5].append(v.reshape(Bd, T, n_heads, LANES))
        outs[6].append(pst[:, 1:, :])

        post_p = functools.partial(_post, yp, oatt, opool, mod_p, g_ffn[l], w_out_b, w_up_b, conv_w[l],
                                   conv_b[l], w_dn_b, None, nseq=1, T=tp)
        if B * (S // tp) == Bd:
            cfg = dict(n_heads=n_heads, T=T, n_chunks=page_table.shape[1] // PAGES_PER_CHUNK,
                       layer_base=l * n_pool, lam_init=lam_init)
            yp, cst, oatt_s = post_p(paged=(page_table, lamvec, q3, kn, vn, g_sub[l], ck, cv, cfg))
        else:
            yp, cst = post_p()
            oatt_s = _atts(q3, kn, vn, ck, cv, page_table, l * n_pool, lamvec, g_sub[l],
                           n_heads=n_heads, lam_init=lam_init)
        outs[3].append(cst[:, CONV_PREFIX_ROWS - (CONV_W - 1):, :])
        ys, cst = _post(ys, oatt_s.reshape(Bd * T, q_w), opool_s, mod_s, g_ffn[l], w_out_b, w_up_b,
                        conv_w[l], conv_b[l], w_dn_b, cpre, nseq=ns_post, T=T)
        outs[7].append(cst[:, CONV_PREFIX_ROWS - (CONV_W - 1):, :])

    return (yp, ys) + tuple(o[0][None] if depth == 1 else jnp.stack(o) for o in outs)
```

```python
import functools
import math

import jax
import jax.numpy as jnp
from jax import lax
from jax.experimental import pallas as pl
from jax.experimental.pallas import tpu as pltpu

F32 = jnp.float32
BF16 = jnp.bfloat16

POOL_WINDOWS = (2, 4, 8, 16)
POOL_PREFIX_ROWS = 16
CONV_W = 3
CONV_PREFIX_ROWS = 8
ROPE_THETA = 500000.0
EPS = 1e-6
LANES = 128
NEG_BIG = -1e30
VMEM_LIMIT = 56 * 1024 * 1024


def _silu(x):
    return x / (1.0 + jnp.exp(-x))


def _rms(x, axis=-1):
    return x * lax.rsqrt(jnp.mean(x * x, axis=axis, keepdims=True) + EPS)


def _dot(a, b):
    return jnp.dot(a, b, preferred_element_type=F32)


def _dot_nt(a, b):
    return lax.dot_general(a, b, (((1,), (1,)), ((), ())), preferred_element_type=F32)


def _split_dot(x, w_bf16):
    hi = x.astype(BF16)
    lo = (x - hi.astype(F32)).astype(BF16)
    return _dot(hi, w_bf16) + _dot(lo, w_bf16)


def _ada_kernel(c_ref, w_ref, b_ref, o_ref):
    s = _silu(c_ref[...]).astype(BF16)
    o_ref[...] = _dot(s, w_ref[...].astype(BF16)) + b_ref[...]


def _ada(c, w_ada, b_ada):
    nb, d = c.shape
    n = w_ada.shape[1]
    bn = 1024 if n % 1024 == 0 else n
    return pl.pallas_call(
        _ada_kernel,
        grid=(n // bn,),
        in_specs=[pl.BlockSpec((nb, d), lambda j: (0, 0)),
                  pl.BlockSpec((d, bn), lambda j: (0, j)),
                  pl.BlockSpec((1, bn), lambda j: (0, j))],
        out_specs=pl.BlockSpec((nb, bn), lambda j: (0, j)),
        out_shape=jax.ShapeDtypeStruct((nb, n), F32),
        compiler_params=pltpu.CompilerParams(dimension_semantics=("arbitrary",),
                                             vmem_limit_bytes=VMEM_LIMIT),
        name="ada",
    )(c, w_ada, b_ada.reshape(1, n))


def _pre_kernel(*refs, nseq, T, n_heads, att_w, pool_w, pos_base, has_prefix):
    if has_prefix:
        (x_ref, mod_ref, gattn_ref, win_ref, gq_ref, gk_ref, gmat_ref, cos_ref, sa_ref, sb_ref,
         wpool_ref, pscale_ref, prefix_ref,
         q_ref, k_ref, kb_ref, v_ref, vb_ref, op_ref, pst_ref) = refs
    else:
        (x_ref, mod_ref, gattn_ref, win_ref, gq_ref, gk_ref, gmat_ref, cos_ref, sa_ref, sb_ref,
         wpool_ref, pscale_ref,
         q_ref, k_ref, kb_ref, v_ref, vb_ref, op_ref, pst_ref) = refs
    j = pl.program_id(1)
    M = nseq * T
    q_w = n_heads * LANES

    x = x_ref[...]
    d = x.shape[-1]
    h = _rms(x) * gattn_ref[...] * (1.0 + mod_ref[:, 1:2, :]) + mod_ref[:, 0:1, :]
    hb = h.reshape(M, d).astype(BF16)
    o_k, o_v, o_u = q_w, 2 * q_w, 2 * q_w + att_w
    pq = _dot(hb, win_ref[:, 0:o_k])
    pk = _dot(hb, win_ref[:, o_k:o_v])
    u = _dot(hb, win_ref[:, o_u:o_u + pool_w])
    v = _dot(hb, win_ref[:, o_v:o_u])

    cos = jnp.broadcast_to(cos_ref[...][None], (nseq, T, LANES)).reshape(M, LANES)
    sa = jnp.broadcast_to(sa_ref[...][None], (nseq, T, LANES)).reshape(M, LANES)
    sb = jnp.broadcast_to(sb_ref[...][None], (nseq, T, LANES)).reshape(M, LANES)

    def qk_norm_rope(t, g):
        ms = _split_dot(t * t, gmat_ref[...])
        t = t * lax.rsqrt(ms + EPS) * g
        outs = []
        for hh in range(n_heads):
            th = t[:, hh * LANES:(hh + 1) * LANES]
            half = 8
            outs.append(th * cos + pltpu.roll(th, LANES - half, axis=1) * sa
                        + pltpu.roll(th, half, axis=1) * sb)
        return jnp.concatenate(outs, axis=1)

    q = qk_norm_rope(pq, gq_ref[...])
    k = qk_norm_rope(pk, gk_ref[...])

    scale = (LANES // 2) ** -0.5 * math.log2(math.e)
    q_ref[...] = (q * scale).astype(BF16)
    kb_ref[...] = k.astype(BF16)
    vb_ref[...] = v.astype(BF16)
    for hh in range(n_heads):
        k_ref[pl.ds(hh, M, stride=n_heads), :] = k[:, hh * LANES:(hh + 1) * LANES]
        v_ref[pl.ds(hh, M, stride=n_heads), :] = v[:, hh * LANES:(hh + 1) * LANES]

    P = POOL_PREFIX_ROWS
    if has_prefix:
        pre = prefix_ref[...]
    else:
        @pl.when(j == 0)
        def _():
            pst_ref[...] = jnp.zeros_like(pst_ref)
        pre = pst_ref[...]
    ext3 = jnp.concatenate([pre, u.reshape(nseq, T, pool_w)], axis=1)
    pst_ref[...] = ext3[:, T:T + P, :]
    ext = ext3.reshape(nseq * (P + T), pool_w)

    pos = pos_base + j * T + lax.broadcasted_iota(jnp.int32, (nseq, T, LANES), 1)
    g_w = pool_w // len(POOL_WINDOWS)
    outs = []
    for g, w in enumerate(POOL_WINDOWS):
        e = ext[:, g * g_w:(g + 1) * g_w]
        s = e
        sh = 1
        while sh < w:
            s = s + pltpu.roll(s, sh, axis=0)
            sh *= 2
        s3 = s.reshape(nseq, P + T, g_w)[:, P:, :]
        e3 = e.reshape(nseq, P + T, g_w)[:, P:, :]
        cnt = jnp.minimum(w, pos + 1).astype(F32)
        pooled = (s3 / cnt - e3).reshape(M, g_w).astype(BF16)
        outs.append(_dot(pooled, wpool_ref[g]))
    op_ref[...] = (jnp.concatenate(outs, axis=1) * pscale_ref[...]).astype(BF16)


def _pre(x, mod, g_attn, w_in, gq_t, gk_t, gmat, cos, sa, sb, w_pool, pool_scale, prefix,
         *, nseq, T, n_heads, pos_base):
    ns_total, t_total, d = x.shape
    in_w = w_in.shape[1]
    pool_w = pool_scale.shape[-1]
    q_w = n_heads * LANES
    att_w = in_w - 2 * q_w - pool_w
    assert pool_w // len(POOL_WINDOWS) == LANES and att_w == q_w
    ni, nj = ns_total // nseq, t_total // T
    M = nseq * T
    n_tok = ns_total * t_total
    has_prefix = prefix is not None
    assert has_prefix or nseq == 1

    const = lambda shape: pl.BlockSpec(shape, lambda i, j: (0,) * len(shape),
                                       pipeline_mode=pl.Buffered(1))
    tok = lambda w: pl.BlockSpec((M, w), lambda i, j: (i * nj + j, 0))
    in_specs = [
        pl.BlockSpec((nseq, T, d), lambda i, j: (i, j, 0)),
        pl.BlockSpec((nseq, 6, d), lambda i, j: (i, 0, 0)),
        const((1, d)), const((d, in_w)), const((1, q_w)), const((1, q_w)), const((q_w, q_w)),
        pl.BlockSpec((T, LANES), lambda i, j: (j, 0)),
        pl.BlockSpec((T, LANES), lambda i, j: (j, 0)),
        pl.BlockSpec((T, LANES), lambda i, j: (j, 0)),
        const(w_pool.shape), const((1, pool_w)),
    ]
    args = [x, mod, g_attn.reshape(1, d), w_in, gq_t, gk_t, gmat, cos, sa, sb,
            w_pool, pool_scale.reshape(1, pool_w)]
    if has_prefix:
        in_specs.append(pl.BlockSpec((nseq, POOL_PREFIX_ROWS, pool_w), lambda i, j: (i, 0, 0)))
        args.append(prefix)
    out_shape = (
        jax.ShapeDtypeStruct((n_tok, q_w), BF16),
        jax.ShapeDtypeStruct((n_tok * n_heads, LANES), F32),
        jax.ShapeDtypeStruct((n_tok, q_w), BF16),
        jax.ShapeDtypeStruct((n_tok * n_heads, LANES), F32),
        jax.ShapeDtypeStruct((n_tok, att_w), BF16),
        jax.ShapeDtypeStruct((n_tok, pool_w), BF16),
        jax.ShapeDtypeStruct((ns_total, POOL_PREFIX_ROWS, pool_w), F32),
    )
    tok_head = pl.BlockSpec((M * n_heads, LANES), lambda i, j: (i * nj + j, 0))
    out_specs = (tok(q_w), tok_head, tok(q_w), tok_head, tok(att_w), tok(pool_w),
                 pl.BlockSpec((nseq, POOL_PREFIX_ROWS, pool_w), lambda i, j: (i, 0, 0)))
    kern = functools.partial(_pre_kernel, nseq=nseq, T=T, n_heads=n_heads, att_w=att_w,
                             pool_w=pool_w, pos_base=pos_base, has_prefix=has_prefix)
    return pl.pallas_call(
        kern, grid=(ni, nj), in_specs=in_specs, out_specs=out_specs, out_shape=out_shape,
        compiler_params=pltpu.CompilerParams(dimension_semantics=("arbitrary", "arbitrary"),
                                             vmem_limit_bytes=VMEM_LIMIT),
        name="pre",
    )(*args)


def _lam_from(lamvec_ref, lam_init):
    a = jnp.sum(lamvec_ref[0:1, :] * lamvec_ref[1:2, :], axis=-1, keepdims=True)
    b = jnp.sum(lamvec_ref[2:3, :] * lamvec_ref[3:4, :], axis=-1, keepdims=True)
    return jnp.exp(a) - jnp.exp(b) + lam_init


def _attp_kernel(lamvec_ref, q_ref, k_ref, v_ref, gsub_ref, o_ref, qs_ref, m_ref, l_ref, acc_ref,
                 *, tq, nq, lam_init):
    half = LANES // 2
    lane = lax.broadcasted_iota(jnp.int32, (tq, LANES), 1)
    for qi in range(nq):
        q = q_ref[qi * tq:(qi + 1) * tq, :]
        zero = jnp.zeros_like(q)
        qs_ref[qi, 0:tq, :] = jnp.where(lane < half, q, zero)
        qs_ref[qi, tq:2 * tq, :] = jnp.where(lane >= half, q, zero)
    m_ref[...] = jnp.full_like(m_ref, NEG_BIG)
    l_ref[...] = jnp.zeros_like(l_ref)
    acc_ref[...] = jnp.zeros_like(acc_ref)

    row = lax.broadcasted_iota(jnp.int32, (2 * tq, tq), 0)
    col = lax.broadcasted_iota(jnp.int32, (2 * tq, tq), 1)
    causal = col <= jnp.where(row >= tq, row - tq, row)

    def scores(qi, ki):
        s = _dot_nt(qs_ref[qi], k_ref[ki * tq:(ki + 1) * tq, :])
        return jnp.where(causal, s, NEG_BIG) if qi == ki else s

    def update(qi, ki, s):
        m_prev = m_ref[qi]
        m_new = jnp.maximum(m_prev, jnp.max(s, axis=1, keepdims=True))
        alpha = jnp.exp2(m_prev - m_new)
        p = jnp.exp2(s - jnp.concatenate([m_new] * (tq // LANES), axis=1))
        l_ref[qi] = alpha * l_ref[qi] + jnp.sum(p, axis=1, keepdims=True)
        acc_ref[qi] = alpha * acc_ref[qi] + _dot(p.astype(BF16), v_ref[ki * tq:(ki + 1) * tq, :])
        m_ref[qi] = m_new

    pairs = [(qi, ki) for ki in range(nq) for qi in range(ki, nq)]
    s = scores(*pairs[0])
    for i, (qi, ki) in enumerate(pairs):
        s_next = scores(*pairs[i + 1]) if i + 1 < len(pairs) else None
        update(qi, ki, s)
        s = s_next

    lam = _lam_from(lamvec_ref, lam_init)
    for qi in range(nq):
        o = acc_ref[qi] / l_ref[qi]
        dlt = o[0:tq, :] - lam * o[tq:2 * tq, :]
        o_ref[qi * tq:(qi + 1) * tq, :] = (_rms(dlt) * gsub_ref[...]
                                           * (1.0 - lam_init)).astype(o_ref.dtype)


def _attp(q, kb, vb, lamvec, g_sub, *, n_seq, seq, n_heads, lam_init):
    tq = 256 if seq % 256 == 0 else seq
    nq = seq // tq
    n_tok = q.shape[0]
    seq_head = pl.BlockSpec((seq, LANES), lambda b, h: (b, h))
    return pl.pallas_call(
        functools.partial(_attp_kernel, tq=tq, nq=nq, lam_init=lam_init),
        grid=(n_seq, n_heads),
        in_specs=[pl.BlockSpec((4, LANES), lambda b, h: (0, 0)),
                  seq_head, seq_head, seq_head,
                  pl.BlockSpec((1, LANES), lambda b, h: (0, 0))],
        out_specs=seq_head,
        out_shape=jax.ShapeDtypeStruct((n_tok, n_heads * LANES), BF16),
        scratch_shapes=[pltpu.VMEM((nq, 2 * tq, LANES), BF16),
                        pltpu.VMEM((nq, 2 * tq, LANES), F32),
                        pltpu.VMEM((nq, 2 * tq, LANES), F32),
                        pltpu.VMEM((nq, 2 * tq, LANES), F32)],
        compiler_params=pltpu.CompilerParams(dimension_semantics=("arbitrary", "arbitrary"),
                                             vmem_limit_bytes=VMEM_LIMIT),
        name="attp",
    )(lamvec, q, kb, vb, g_sub.reshape(1, LANES))


PAGES_PER_CHUNK = 8
KV_RING_SLOTS = 4


def _lane_tile(x, n):
    return x if n == 1 else jnp.concatenate([x] * n, axis=1)


def _stack_q(q, n_heads, T):
    half = LANES // 2
    lane = lax.broadcasted_iota(jnp.int32, (T, LANES), 1)
    rows = []
    for hh in range(n_heads):
        qh = q[:, hh * LANES:(hh + 1) * LANES]
        zero = jnp.zeros_like(qh)
        rows += [jnp.where(lane < half, qh, zero), jnp.where(lane >= half, qh, zero)]
    return jnp.concatenate(rows, axis=0)


def _softmax_partial(s, v):
    mj = jnp.broadcast_to(jnp.max(s, axis=1, keepdims=True), (s.shape[0], LANES))
    p = jnp.exp2(s - _lane_tile(mj, s.shape[1] // LANES))
    return mj, jnp.sum(p, axis=1, keepdims=True), _dot(p.astype(BF16), v)


def _merge_partials(state, parts):
    m_prev, l_prev, acc_prev = state
    m_new = m_prev
    for mj, _, _ in parts:
        m_new = jnp.maximum(m_new, mj)
    alpha = jnp.exp2(m_prev - m_new)
    l_new = alpha * l_prev
    acc = alpha * acc_prev
    for mj, lj, oj in parts:
        wj = jnp.exp2(mj - m_new)
        l_new = l_new + wj * lj
        acc = acc + wj * oj
    return m_new, l_new, acc


def _page_copies(pt_ref, ck_hbm, cv_hbm, kbuf, vbuf, sems, b, chunk, slot, *, layer_base):
    copies = []
    for jj in range(PAGES_PER_CHUNK):
        page = layer_base + pt_ref[b, chunk * PAGES_PER_CHUNK + jj]
        copies.append(pltpu.make_async_copy(ck_hbm.at[page], kbuf.at[slot, jj], sems.at[0, slot]))
        copies.append(pltpu.make_async_copy(cv_hbm.at[page], vbuf.at[slot, jj], sems.at[1, slot]))
    return copies


def _chunk_scores(qall, kbuf, slot):
    return [_dot_nt(qall, kbuf[slot, jj].astype(BF16)) for jj in range(PAGES_PER_CHUNK)]


def _chunk_accumulate(scores, vbuf, slot, head_ok, state):
    parts = [_softmax_partial(jnp.where(head_ok, s, NEG_BIG), vbuf[slot, jj].astype(BF16))
             for jj, s in enumerate(scores)]
    return _merge_partials(state, parts)


def _attend_new_and_finish(qall, state, kn_ref, vn_ref, lamvec_ref, gsub_ref, o_ref,
                           *, n_heads, T, lam_init):
    R, n_new = qall.shape[0], kn_ref.shape[0]
    nrow = lax.broadcasted_iota(jnp.int32, (R, n_new), 0)
    ncol = lax.broadcasted_iota(jnp.int32, (R, n_new), 1)
    ok = ((ncol % n_heads) == (nrow // (2 * T))) & ((ncol // n_heads) <= (nrow % T)) \
        & (ncol < T * n_heads)
    s = jnp.where(ok, _dot_nt(qall, kn_ref[...]), NEG_BIG)
    _, l, acc = _merge_partials(state, [_softmax_partial(s, vn_ref[...])])
    lam = _lam_from(lamvec_ref, lam_init)
    o = acc / l
    for hh in range(n_heads):
        dlt = o[(2 * hh) * T:(2 * hh + 1) * T, :] - lam * o[(2 * hh + 1) * T:(2 * hh + 2) * T, :]
        o_ref[:, hh * LANES:(hh + 1) * LANES] = _rms(dlt) * gsub_ref[...] * (1.0 - lam_init)


def _paged_attention_step(b, n_seq, pt_ref, lamvec_ref, q_ref, kn_ref, vn_ref, gsub_ref, ck_hbm, cv_hbm,
                          o_ref, kbuf, vbuf, sems, *, n_heads, T, n_chunks, layer_base, lam_init,
                          while_in_flight=None, state_ref=None):
    assert n_chunks % KV_RING_SLOTS == 0
    ahead = KV_RING_SLOTS - 1
    copies = functools.partial(_page_copies, pt_ref, ck_hbm, cv_hbm, kbuf, vbuf, sems,
                               layer_base=layer_base)

    @pl.when(b == 0)
    def _():
        for c in range(ahead):
            for cp in copies(0, c, c):
                cp.start()

    R = n_heads * 2 * T
    rows_page = kbuf.shape[2]
    row = lax.broadcasted_iota(jnp.int32, (R, rows_page), 0)
    col = lax.broadcasted_iota(jnp.int32, (R, rows_page), 1)
    head_ok = (col % n_heads) == (row // (2 * T))
    qall = _stack_q(q_ref[...], n_heads, T).astype(BF16)
    state = (jnp.full((R, LANES), NEG_BIG, F32), jnp.zeros((R, LANES), F32), jnp.zeros((R, LANES), F32))
    if state_ref is not None:
        for i in range(3):
            state_ref[i] = state[i]

    def attend(c, state):
        for cp in copies(b, c, c % KV_RING_SLOTS):
            cp.wait()
        scores = _chunk_scores(qall, kbuf, c % KV_RING_SLOTS)
        return _chunk_accumulate(scores, vbuf, c % KV_RING_SLOTS, head_ok, state)

    for c in range(n_chunks):
        nxt = c + ahead
        if nxt < n_chunks:
            for cp in copies(b, nxt, nxt % KV_RING_SLOTS):
                cp.start()
        else:
            @pl.when(b + 1 < n_seq)
            def _():
                for cp in copies(b + 1, nxt - n_chunks, nxt % KV_RING_SLOTS):
                    cp.start()
        if while_in_flight is not None:
            while_in_flight(c)
        if state_ref is None:
            state = attend(c, state)
        else:
            @pl.when(b < n_seq)
            def _():
                new = attend(c, tuple(state_ref[i] for i in range(3)))
                for i in range(3):
                    state_ref[i] = new[i]
    if state_ref is not None:
        state = tuple(state_ref[i] for i in range(3))
    _attend_new_and_finish(qall, state, kn_ref, vn_ref, lamvec_ref, gsub_ref, o_ref,
                           n_heads=n_heads, T=T, lam_init=lam_init)


def _atts_kernel(pt_ref, lamvec_ref, q_ref, kn_ref, vn_ref, gsub_ref, ck_hbm, cv_hbm, o_ref,
                 kbuf, vbuf, sems, **kw):
    _paged_attention_step(pl.program_id(0), pl.num_programs(0), pt_ref, lamvec_ref, q_ref, kn_ref,
                          vn_ref, gsub_ref, ck_hbm, cv_hbm, o_ref, kbuf, vbuf, sems, **kw)


def _kv_ring_scratch(rows_page):
    return [pltpu.VMEM((KV_RING_SLOTS, PAGES_PER_CHUNK, rows_page, LANES), F32),
            pltpu.VMEM((KV_RING_SLOTS, PAGES_PER_CHUNK, rows_page, LANES), F32),
            pltpu.SemaphoreType.DMA((2, KV_RING_SLOTS))]


def _atts(q3, kn, vn, cache_k, cache_v, page_table, layer_base, lamvec, g_sub,
          *, n_heads, lam_init):
    bd, T, qw = q3.shape
    n_pages = page_table.shape[1]
    rows_page = cache_k.shape[1]
    n_new = kn.shape[1]
    assert n_pages % (PAGES_PER_CHUNK * KV_RING_SLOTS) == 0
    in_specs = [
        pl.BlockSpec((4, LANES), lambda b, pt: (0, 0)),
        pl.BlockSpec((None, T, qw), lambda b, pt: (b, 0, 0)),
        pl.BlockSpec((None, n_new, LANES), lambda b, pt: (b, 0, 0)),
        pl.BlockSpec((None, n_new, LANES), lambda b, pt: (b, 0, 0)),
        pl.BlockSpec((1, LANES), lambda b, pt: (0, 0)),
        pl.BlockSpec(memory_space=pl.ANY),
        pl.BlockSpec(memory_space=pl.ANY),
    ]
    grid_spec = pltpu.PrefetchScalarGridSpec(
        num_scalar_prefetch=1, grid=(bd,), in_specs=in_specs,
        out_specs=pl.BlockSpec((None, T, qw), lambda b, pt: (b, 0, 0)),
        scratch_shapes=_kv_ring_scratch(rows_page),
    )
    kern = functools.partial(_atts_kernel, n_heads=n_heads, T=T,
                             n_chunks=n_pages // PAGES_PER_CHUNK, layer_base=layer_base,
                             lam_init=lam_init)
    return pl.pallas_call(
        kern, grid_spec=grid_spec,
        out_shape=jax.ShapeDtypeStruct((bd, T, qw), F32),
        compiler_params=pltpu.CompilerParams(dimension_semantics=("arbitrary",),
                                             vmem_limit_bytes=VMEM_LIMIT),
        name="atts",
    )(page_table, lamvec, q3, kn, vn, g_sub.reshape(1, LANES), cache_k, cache_v)


def _post_kernel(*refs, nseq, T, fc, has_prefix, paged):
    refs = list(refs)
    pt_ref = refs.pop(0) if paged else None
    x_ref, oa_ref, op_ref, mod_ref, gffn_ref, wout_ref, wup_ref, cw_ref, cb_ref, wdn_ref = refs[:10]
    del refs[:10]
    cpre_ref = refs.pop(0) if has_prefix else None
    if paged:
        lamvec_ref, q_ref, kn_ref, vn_ref, gsub_ref, ck_hbm, cv_hbm = refs[:7]
        del refs[:7]
    y_ref, cst_ref = refs[:2]
    del refs[:2]
    oatt_ref = refs.pop(0) if paged else None
    h2_ref, a_ref, hm_ref = refs[:3]
    del refs[:3]
    if paged:
        kbuf, vbuf, sems, state_ref = refs
    if not has_prefix:
        cpre_ref = cst_ref
    j = pl.program_id(1)
    M = nseq * T
    d = x_ref.shape[-1]
    ff = wdn_ref.shape[0]
    att_w = oa_ref.shape[1]
    C = CONV_PREFIX_ROWS

    mix = _dot(oa_ref[...].astype(BF16), wout_ref[0:att_w, :]) \
        + _dot(op_ref[...].astype(BF16), wout_ref[att_w:, :])
    x1 = x_ref[...] + mod_ref[:, 2:3, :] * mix.reshape(nseq, T, d)
    y_ref[...] = x1
    h2 = _rms(x1) * gffn_ref[...] * (1.0 + mod_ref[:, 4:5, :]) + mod_ref[:, 3:4, :]
    h2_ref[...] = h2.reshape(M, d).astype(BF16)

    if not has_prefix:
        @pl.when(j == 0)
        def _():
            cst_ref[...] = jnp.zeros_like(cst_ref)

    def conv(a, col):
        pre = cpre_ref[:, :, col:col + fc]
        ext3 = jnp.concatenate([pre, a.reshape(nseq, T, fc)], axis=1)
        cst_ref[:, :, col:col + fc] = ext3[:, T:T + C, :]
        ext = ext3.reshape(nseq * (C + T), fc)
        y = (cb_ref[:, col:col + fc]
             + pltpu.roll(ext, 2, axis=0) * cw_ref[0:1, col:col + fc]
             + pltpu.roll(ext, 1, axis=0) * cw_ref[1:2, col:col + fc]
             + ext * cw_ref[2:3, col:col + fc])
        return y.reshape(nseq, C + T, fc)[:, C:, :].reshape(M, fc)

    nc = ff // fc

    def up_proj(ci):
        hb = h2_ref[...]
        a_ref[ci % 2, :, 0:fc] = _dot(hb, wup_ref[:, ci * fc:(ci + 1) * fc])
        a_ref[ci % 2, :, fc:2 * fc] = _dot(hb, wup_ref[:, ff + ci * fc:ff + (ci + 1) * fc])

    def conv_gate(ci):
        g = conv(a_ref[ci % 2, :, 0:fc], ci * fc)
        up = conv(a_ref[ci % 2, :, fc:2 * fc], ff + ci * fc)
        hm_ref[:, ci * fc:(ci + 1) * fc] = (_silu(g) * up).astype(BF16)

    def ffn_chunks(lo, hi):
        for ci in range(lo, hi):
            if ci + 1 < nc:
                up_proj(ci + 1)
            conv_gate(ci)

    up_proj(0)
    if paged:
        half_ffn, half_att = (nc + 1) // 2, paged["n_chunks"] // 2
        runs = {0: (0, half_ffn), half_att: (half_ffn, nc)}
        step = pl.program_id(0) * pl.num_programs(1) + j
        n_steps = pl.num_programs(0) * pl.num_programs(1)
        _paged_attention_step(step, n_steps, pt_ref, lamvec_ref, q_ref, kn_ref, vn_ref, gsub_ref,
                              ck_hbm, cv_hbm, oatt_ref, kbuf, vbuf, sems,
                              while_in_flight=lambda c: ffn_chunks(*runs[c]) if c in runs else None,
                              state_ref=state_ref, **paged)
    else:
        ffn_chunks(0, nc)
    f = _dot(hm_ref[...], wdn_ref[...])
    y_ref[...] = y_ref[...] + mod_ref[:, 5:6, :] * f.reshape(nseq, T, d)


def _post(x, oa, op, mod, g_ffn, w_out, w_up, conv_w, conv_b, w_down, cprefix, *, nseq, T,
          paged=None):
    ns_total, t_total, d = x.shape
    ff = w_down.shape[0]
    fc = 256 if ff % 256 == 0 else ff
    ni, nj = ns_total // nseq, t_total // T
    M = nseq * T
    has_prefix = cprefix is not None
    assert has_prefix or nseq == 1
    C = CONV_PREFIX_ROWS

    const = lambda shape: pl.BlockSpec(shape, lambda i, j, *_: (0,) * len(shape),
                                       pipeline_mode=pl.Buffered(1))
    tok = lambda w: pl.BlockSpec((M, w), lambda i, j, *_: (i * nj + j, 0))
    in_specs = [
        pl.BlockSpec((nseq, T, d), lambda i, j, *_: (i, j, 0)),
        tok(oa.shape[1]), tok(op.shape[1]),
        pl.BlockSpec((nseq, 6, d), lambda i, j, *_: (i, 0, 0)),
        const((1, d)), const(w_out.shape), const(w_up.shape), const((CONV_W, 2 * ff)),
        const((1, 2 * ff)), const(w_down.shape),
    ]
    args = [x, oa, op, mod, g_ffn.reshape(1, d), w_out, w_up, conv_w, conv_b.reshape(1, 2 * ff), w_down]
    if has_prefix:
        in_specs.append(pl.BlockSpec((nseq, C, 2 * ff), lambda i, j, *_: (i, 0, 0)))
        args.append(cprefix)
    out_specs = [pl.BlockSpec((nseq, T, d), lambda i, j, *_: (i, j, 0)),
                 pl.BlockSpec((nseq, C, 2 * ff), lambda i, j, *_: (i, 0, 0))]
    out_shape = [jax.ShapeDtypeStruct(x.shape, F32),
                 jax.ShapeDtypeStruct((ns_total, C, 2 * ff), F32)]
    scratch = [pltpu.VMEM((M, d), BF16),
               pltpu.VMEM((2, M, 2 * fc), F32),
               pltpu.VMEM((M, ff), BF16)]
    prefetch = []
    cfg = None
    if paged is not None:
        page_table, lamvec, q3, kn, vn, g_sub, cache_k, cache_v, cfg = paged
        bd, t_s, qw = q3.shape
        assert bd == ni * nj
        seq_blk = lambda r, w: pl.BlockSpec((None, r, w), lambda i, j, *_: (i * nj + j, 0, 0))
        in_specs += [pl.BlockSpec((4, LANES), lambda i, j, *_: (0, 0)),
                     seq_blk(t_s, qw), seq_blk(kn.shape[1], LANES), seq_blk(vn.shape[1], LANES),
                     pl.BlockSpec((1, LANES), lambda i, j, *_: (0, 0)),
                     pl.BlockSpec(memory_space=pl.ANY), pl.BlockSpec(memory_space=pl.ANY)]
        args += [lamvec, q3, kn, vn, g_sub.reshape(1, LANES), cache_k, cache_v]
        out_specs.append(seq_blk(t_s, qw))
        out_shape.append(jax.ShapeDtypeStruct((bd, t_s, qw), F32))
        scratch += _kv_ring_scratch(cache_k.shape[1])
        scratch.append(pltpu.VMEM((3, cfg["n_heads"] * 2 * t_s, LANES), F32))
        prefetch = [page_table]
    grid_spec = pltpu.PrefetchScalarGridSpec(
        num_scalar_prefetch=len(prefetch), grid=(ni, nj), in_specs=in_specs, out_specs=out_specs,
        scratch_shapes=scratch)
    return pl.pallas_call(
        functools.partial(_post_kernel, nseq=nseq, T=T, fc=fc, has_prefix=has_prefix, paged=cfg),
        grid_spec=grid_spec, out_shape=out_shape,
        compiler_params=pltpu.CompilerParams(dimension_semantics=("arbitrary", "arbitrary"),
                                             vmem_limit_bytes=VMEM_LIMIT),
        name="post",
    )(*prefetch, *args)


def _rope_tables(pos, d_qk, rot_dim):
    half = rot_dim // 2
    inv = ROPE_THETA ** (-jnp.arange(0, rot_dim, 2, dtype=F32) / rot_dim)
    ang = pos.astype(F32)[:, None] * inv[None, :]
    cos, sin = jnp.cos(ang), jnp.sin(ang)
    l64 = jnp.arange(LANES) % d_qk
    idx = l64 % half
    cos_t = jnp.where(l64 < rot_dim, cos[:, idx], 1.0)
    sa = jnp.where(l64 < half, -sin[:, idx], 0.0)
    sb = jnp.where((l64 >= half) & (l64 < rot_dim), sin[:, idx], 0.0)
    return cos_t, sa, sb


def _tile_rows(n, target):
    t = min(n, target)
    while n % t:
        t //= 2
    return t


def kernel(x_prompt, x_sample, cache_k, cache_v, state_pool, state_conv, page_table, c_prompt, c_sample, w_ada, b_ada, g_attn, w_in, g_q, g_k, lam_q1, lam_k1, lam_q2, lam_k2, g_sub, w_pool, pool_scale, w_out, g_ffn, w_up, conv_w, conv_b, w_down):
    B, S, D = x_prompt.shape
    Bd, T, _ = x_sample.shape
    depth, n_pool, page_size, n_heads, dk2 = cache_k.shape
    d_qk = g_q.shape[-1]
    assert dk2 == 2 * d_qk == LANES and cache_v.shape[-1] == LANES
    rot_dim = d_qk // 4
    assert rot_dim == 16
    past_len = page_table.shape[1] * page_size
    pool_w = pool_scale.shape[-1]
    ff = w_down.shape[1]
    q_w = n_heads * LANES

    ck = cache_k.reshape(depth * n_pool, page_size * n_heads, LANES)
    cv = cache_v.reshape(depth * n_pool, page_size * n_heads, LANES)

    cos_p, sa_p, sb_p = _rope_tables(jnp.arange(S), d_qk, rot_dim)
    cos_s, sa_s, sb_s = _rope_tables(past_len + jnp.arange(T), d_qk, rot_dim)
    blk = jnp.arange(q_w) // d_qk
    gmat = jnp.where(blk[:, None] == blk[None, :], 1.0 / d_qk, 0.0).astype(BF16)

    tp = _tile_rows(S, 512)
    ns = _tile_rows(Bd, max(1, 512 // T))
    ns_post = _tile_rows(Bd, 16)
    n_new = max(LANES, T * n_heads)

    yp, ys = x_prompt, x_sample
    outs = [[] for _ in range(8)]
    for l in range(depth):
        lam_init = 0.8 - 0.6 * math.exp(-0.3 * l)
        lamvec = jnp.pad(jnp.stack([lam_q1[l], lam_k1[l], lam_q2[l], lam_k2[l]]).astype(F32),
                         ((0, 0), (0, LANES - d_qk)))
        w_in_b, w_out_b = w_in[l].astype(BF16), w_out[l].astype(BF16)
        w_up_b, w_dn_b, w_pool_b = w_up[l].astype(BF16), w_down[l].astype(BF16), w_pool[l].astype(BF16)
        gq_t = jnp.tile(g_q[l], q_w // d_qk).reshape(1, q_w)
        gk_t = jnp.tile(g_k[l], q_w // d_qk).reshape(1, q_w)

        mod = _ada(jnp.concatenate([c_prompt, c_sample], axis=0), w_ada[l], b_ada[l])
        mod_p = mod[:B].reshape(B, 6, D)
        mod_s = mod[B:].reshape(Bd, 6, D)

        q, k, kb, v, vb, opool, pst = _pre(
            yp, mod_p, g_attn[l], w_in_b, gq_t, gk_t, gmat, cos_p, sa_p, sb_p, w_pool_b,
            pool_scale[l], None, nseq=1, T=tp, n_heads=n_heads, pos_base=0)
        oatt = _attp(q, kb, vb, lamvec, g_sub[l], n_seq=B, seq=S, n_heads=n_heads, lam_init=lam_init)
        outs[0].append(k.reshape(B, S, n_heads, LANES))
        outs[1].append(v.reshape(B, S, n_heads, LANES))
        outs[2].append(pst[:, 1:, :])

        ppre = jnp.pad(state_pool[l], ((0, 0), (POOL_PREFIX_ROWS - state_pool.shape[2], 0), (0, 0)))
        cpre = jnp.pad(state_conv[l], ((0, 0), (CONV_PREFIX_ROWS - state_conv.shape[2], 0), (0, 0)))
        q, k, kb, v, vb, opool_s, pst = _pre(
            ys, mod_s, g_attn[l], w_in_b, gq_t, gk_t, gmat, cos_s, sa_s, sb_s, w_pool_b,
            pool_scale[l], ppre, nseq=ns, T=T, n_heads=n_heads, pos_base=past_len)
        kn = jnp.pad(kb.reshape(Bd, T * n_heads, LANES), ((0, 0), (0, n_new - T * n_heads), (0, 0)))
        vn = jnp.pad(vb.reshape(Bd, T * n_heads, LANES), ((0, 0), (0, n_new - T * n_heads), (0, 0)))
        q3 = q.astype(F32).reshape(Bd, T, q_w)
        outs[4].append(k.reshape(Bd, T, n_heads, LANES))
        outs[5].append(v.reshape(Bd, T, n_heads, LANES))
        outs[6].append(pst[:, 1:, :])

        post_p = functools.partial(_post, yp, oatt, opool, mod_p, g_ffn[l], w_out_b, w_up_b, conv_w[l],
                                   conv_b[l], w_dn_b, None, nseq=1, T=tp)
        if B * (S // tp) == Bd:
            cfg = dict(n_heads=n_heads, T=T, n_chunks=page_table.shape[1] // PAGES_PER_CHUNK,
                       layer_base=l * n_pool, lam_init=lam_init)
            yp, cst, oatt_s = post_p(paged=(page_table, lamvec, q3, kn, vn, g_sub[l], ck, cv, cfg))
        else:
            yp, cst = post_p()
            oatt_s = _atts(q3, kn, vn, ck, cv, page_table, l * n_pool, lamvec, g_sub[l],
                           n_heads=n_heads, lam_init=lam_init)
        outs[3].append(cst[:, CONV_PREFIX_ROWS - (CONV_W - 1):, :])
        ys, cst = _post(ys, oatt_s.reshape(Bd * T, q_w), opool_s, mod_s, g_ffn[l], w_out_b, w_up_b,
                        conv_w[l], conv_b[l], w_dn_b, cpre, nseq=ns_post, T=T)
        outs[7].append(cst[:, CONV_PREFIX_ROWS - (CONV_W - 1):, :])

    return (yp, ys) + tuple(o[0][None] if depth == 1 else jnp.stack(o) for o in outs)
```
